```python
import math
import jax, jax.numpy as jnp
from jax import lax
import numpy as np

D_MODEL = 1024
BATCH = 4
SEQ = 8192
DEPTH = 2

D_MIX = D_MODEL
SSM_WIDTH = D_MIX // 4
SSM_GROUP = 16
SSM_GROUPS = SSM_WIDTH // SSM_GROUP
SSM_STATE = 64
DIFF_HEAD_DIM = 64
DIFF_V_DIM = 2 * DIFF_HEAD_DIM
DIFF_WIDTH = D_MIX // 2
DIFF_HEADS = DIFF_WIDTH // DIFF_V_DIM
MOBA_HEAD_DIM = 64
MOBA_WIDTH = D_MIX // 4
MOBA_HEADS = MOBA_WIDTH // MOBA_HEAD_DIM
IN_WIDTH = SSM_WIDTH + 3 * DIFF_WIDTH + 3 * MOBA_WIDTH
D_FF = 4 * D_MODEL
ROPE_THETA = 500000.0
ROPE_FRACTION = 4
MOBA_BLOCK = 256
MOBA_TOPK = 3
MOBA_Q_CHUNK = 64
ATTN_Q_BLOCK = 128
EPS = 1e-6
NEG = -1e30

kernel_name = "hybrid_s5_diffattn_moba_adaln_block"


def rmsnorm(x, g):
    xf = x.astype(jnp.float32)
    y = xf * lax.rsqrt(jnp.mean(xf * xf, axis=-1, keepdims=True) + EPS)
    return (y * g.astype(jnp.float32)).astype(x.dtype)


def rope_tables(positions, head_dim):
    rot = head_dim // ROPE_FRACTION
    inv = ROPE_THETA ** (-jnp.arange(0, rot, 2, dtype=jnp.float32) / rot)
    ang = positions.astype(jnp.float32)[..., None] * inv
    return jnp.cos(ang), jnp.sin(ang)


def apply_partial_rope(x, cos, sin):
    half = cos.shape[-1]
    x1 = x[..., :half].astype(jnp.float32)
    x2 = x[..., half:2 * half].astype(jnp.float32)
    r1 = x1 * cos - x2 * sin
    r2 = x2 * cos + x1 * sin
    return jnp.concatenate([r1.astype(x.dtype), r2.astype(x.dtype), x[..., 2 * half:]], axis=-1)


def _complex_affine_combine(e1, e2):
    ar1, ai1, br1, bi1 = e1
    ar2, ai2, br2, bi2 = e2
    ar = ar2 * ar1 - ai2 * ai1
    ai = ar2 * ai1 + ai2 * ar1
    br = ar2 * br1 - ai2 * bi1 + br2
    bi = ar2 * bi1 + ai2 * br1 + bi2
    return (ar, ai, br, bi)


def s5_mixer(u, a_re, a_im, log_dt, b_re, b_im, c_re, c_im, d_skip, glu_w, glu_b, norm_g):
    bsz, s, _ = u.shape
    uf = u.astype(jnp.float32)
    ug = uf.reshape(bsz, s, SSM_GROUPS, SSM_GROUP)
    dt = jnp.exp(log_dt.astype(jnp.float32))[:, None]
    ar = a_re.astype(jnp.float32)
    ai = a_im.astype(jnp.float32)
    mag = jnp.exp(ar * dt)
    abar_re = mag * jnp.cos(ai * dt)
    abar_im = mag * jnp.sin(ai * dt)
    den = ar * ar + ai * ai
    nr = abar_re - 1.0
    ni = abar_im
    kr = ((nr * ar + ni * ai) / den)[..., None]
    ki = ((ni * ar - nr * ai) / den)[..., None]
    br = b_re.astype(jnp.float32)
    bi = b_im.astype(jnp.float32)
    bbar_re = kr * br - ki * bi
    bbar_im = kr * bi + ki * br
    bu_re = jnp.einsum('bsgc,gpc->bsgp', ug, bbar_re)
    bu_im = jnp.einsum('bsgc,gpc->bsgp', ug, bbar_im)
    shp = bu_re.shape
    elems = (jnp.broadcast_to(abar_re, shp), jnp.broadcast_to(abar_im, shp), bu_re, bu_im)
    _, _, h_re, h_im = lax.associative_scan(_complex_affine_combine, elems, axis=1)
    y = (jnp.einsum('bsgp,gcp->bsgc', h_re, c_re.astype(jnp.float32))
         - jnp.einsum('bsgp,gcp->bsgc', h_im, c_im.astype(jnp.float32)))
    y = y.reshape(bsz, s, SSM_WIDTH) + d_skip.astype(jnp.float32) * uf
    y = jax.nn.gelu(y)
    y = y * jax.nn.sigmoid(y @ glu_w.astype(jnp.float32) + glu_b.astype(jnp.float32))
    return rmsnorm(y, norm_g).astype(u.dtype)


def diff_attention(q, k, v, lam, lam_init, subln_g):
    bsz, s, nh, _, dh = q.shape
    dv = v.shape[-1]
    nq = s // ATTN_Q_BLOCK
    kh = jnp.transpose(k, (0, 2, 3, 1, 4))
    vh = jnp.transpose(v, (0, 2, 1, 3))
    qb = jnp.transpose(q, (0, 2, 3, 1, 4)).reshape(bsz, nh, 2, nq, ATTN_Q_BLOCK, dh)
    qb = jnp.moveaxis(qb, 3, 0)
    kpos = jnp.arange(s)
    scale = dh ** -0.5

    def block(args):
        qblk, i = args
        qpos = i * ATTN_Q_BLOCK + jnp.arange(ATTN_Q_BLOCK)
        sc = jnp.einsum('bhcqd,bhckd->bhcqk', qblk, kh).astype(jnp.float32) * scale
        sc = jnp.where(kpos[None, :] <= qpos[:, None], sc, NEG)
        p = jax.nn.softmax(sc, axis=-1)
        a = p[:, :, 0] - lam * p[:, :, 1]
        return jnp.einsum('bhqk,bhkd->bhqd', a, vh.astype(jnp.float32))

    o = lax.map(block, (qb, jnp.arange(nq)))
    o = jnp.moveaxis(o, 0, 2).reshape(bsz, nh, s, dv)
    o = jnp.transpose(o, (0, 2, 1, 3))
    o = rmsnorm(o, subln_g) * (1.0 - lam_init)
    return o.reshape(bsz, s, nh * dv).astype(v.dtype)


def moba_attention(q, k, v):
    bsz, s, nh, dh = q.shape
    nb = -(-s // MOBA_BLOCK)
    s_pad = nb * MOBA_BLOCK
    pad = ((0, 0), (0, s_pad - s), (0, 0), (0, 0))
    qh = jnp.transpose(jnp.pad(q, pad), (0, 2, 1, 3))
    kh = jnp.transpose(jnp.pad(k, pad), (0, 2, 1, 3))
    vh = jnp.transpose(jnp.pad(v, pad), (0, 2, 1, 3))
    kb = kh.reshape(bsz, nh, nb, MOBA_BLOCK, dh)
    vb = vh.reshape(bsz, nh, nb, MOBA_BLOCK, dh)
    kmean = jnp.mean(kb.astype(jnp.float32), axis=3)
    gate = jnp.einsum('bhsd,bhnd->bhsn', qh.astype(jnp.float32), kmean)
    qblock = jnp.arange(s_pad) // MOBA_BLOCK
    past = jnp.arange(nb)[None, :] < qblock[:, None]
    gate = jnp.where(past, gate, NEG)
    topk = min(MOBA_TOPK, nb)
    _, sel = lax.top_k(gate, topk)
    nc = s_pad // MOBA_Q_CHUNK
    q_chunks = jnp.moveaxis(qh.reshape(bsz, nh, nc, MOBA_Q_CHUNK, dh), 2, 0)
    sel_chunks = jnp.moveaxis(sel.reshape(bsz, nh, nc, MOBA_Q_CHUNK, topk), 2, 0)
    bi = jnp.arange(bsz)[:, None, None, None]
    hi = jnp.arange(nh)[None, :, None, None]
    scale = dh ** -0.5

    def chunk(args):
        qc, selc, ci = args
        qpos = ci * MOBA_Q_CHUNK + jnp.arange(MOBA_Q_CHUNK)
        own = (ci * MOBA_Q_CHUNK) // MOBA_BLOCK
        k_sel = kb[bi, hi, selc]
        v_sel = vb[bi, hi, selc]
        s_sel = jnp.einsum('bhqd,bhqnkd->bhqnk', qc, k_sel).astype(jnp.float32) * scale
        s_sel = jnp.where((selc < own)[..., None], s_sel, NEG)
        k_own = lax.dynamic_index_in_dim(kb, own, axis=2, keepdims=False)
        v_own = lax.dynamic_index_in_dim(vb, own, axis=2, keepdims=False)
        s_own = jnp.einsum('bhqd,bhkd->bhqk', qc, k_own).astype(jnp.float32) * scale
        kpos = own * MOBA_BLOCK + jnp.arange(MOBA_BLOCK)
        s_own = jnp.where(kpos[None, :] <= qpos[:, None], s_own, NEG)
        nsel = topk * MOBA_BLOCK
        s_all = jnp.concatenate([s_sel.reshape(bsz, nh, MOBA_Q_CHUNK, nsel), s_own], axis=-1)
        p = jax.nn.softmax(s_all, axis=-1)
        p_sel = p[..., :nsel].reshape(bsz, nh, MOBA_Q_CHUNK, topk, MOBA_BLOCK)
        p_own = p[..., nsel:]
        return (jnp.einsum('bhqnk,bhqnkd->bhqd', p_sel, v_sel.astype(jnp.float32))
                + jnp.einsum('bhqk,bhkd->bhqd', p_own, v_own.astype(jnp.float32)))

    o = lax.map(chunk, (q_chunks, sel_chunks, jnp.arange(nc)))
    o = jnp.moveaxis(o, 0, 2).reshape(bsz, nh, s_pad, dh)[:, :, :s]
    return jnp.transpose(o, (0, 2, 1, 3)).astype(v.dtype)


def setup_inputs(seed: int = 0) -> dict:
    key = jax.random.key(seed)
    ks = jax.random.split(key, 32)
    f32 = jnp.float32
    L, D, G, P = DEPTH, D_MODEL, SSM_GROUPS, SSM_STATE

    def nrm(k, shape, scale):
        return jax.random.normal(k, shape, f32) * scale

    def gain(k, shape):
        return 1.0 + 0.05 * jax.random.normal(k, shape, f32)

    x = jax.random.normal(ks[0], (BATCH, SEQ, D), f32)
    c = jax.random.normal(ks[1], (BATCH, D), f32)
    offs = jax.random.randint(ks[2], (BATCH, 1), 0, 4096, dtype=jnp.int32)
    positions = jnp.arange(SEQ, dtype=jnp.int32)[None, :] + offs
    a_re = -0.5 * jnp.exp(0.05 * jax.random.normal(ks[3], (L, G, P), f32))
    a_im = (math.pi * jnp.arange(P, dtype=f32))[None, None, :] + 0.01 * jax.random.normal(ks[4], (L, G, P), f32)
    log_dt = jax.random.uniform(ks[5], (L, G), f32, math.log(0.001), math.log(0.1))
    return {
        "x": x,
        "c": c,
        "positions": positions,
        "norm1_g": gain(ks[6], (L, D)),
        "norm2_g": gain(ks[7], (L, D)),
        "w_ada": nrm(ks[8], (L, D, 6 * D), 0.5 * D ** -0.5),
        "b_ada": nrm(ks[9], (L, 6 * D), 0.02),
        "w_in": nrm(ks[10], (L, D, IN_WIDTH), D ** -0.5),
        "w_out": nrm(ks[11], (L, D_MIX, D), D_MIX ** -0.5),
        "ssm_a_re": a_re,
        "ssm_a_im": a_im,
        "ssm_log_dt": log_dt,
        "ssm_b_re": nrm(ks[12], (L, G, P, SSM_GROUP), (2.0 * SSM_GROUP) ** -0.5),
        "ssm_b_im": nrm(ks[13], (L, G, P, SSM_GROUP), (2.0 * SSM_GROUP) ** -0.5),
        "ssm_c_re": nrm(ks[14], (L, G, SSM_GROUP, P), (2.0 * P) ** -0.5),
        "ssm_c_im": nrm(ks[15], (L, G, SSM_GROUP, P), (2.0 * P) ** -0.5),
        "ssm_d": nrm(ks[16], (L, SSM_WIDTH), 1.0),
        "ssm_glu_w": nrm(ks[17], (L, SSM_WIDTH, SSM_WIDTH), SSM_WIDTH ** -0.5),
        "ssm_glu_b": nrm(ks[18], (L, SSM_WIDTH), 0.02),
        "ssm_norm_g": gain(ks[19], (L, SSM_WIDTH)),
        "diff_lq1": nrm(ks[20], (L, DIFF_HEAD_DIM), 0.1),
        "diff_lk1": nrm(ks[21], (L, DIFF_HEAD_DIM), 0.1),
        "diff_lq2": nrm(ks[22], (L, DIFF_HEAD_DIM), 0.1),
        "diff_lk2": nrm(ks[23], (L, DIFF_HEAD_DIM), 0.1),
        "diff_subln_g": gain(ks[24], (L, DIFF_V_DIM)),
        "moba_norm_g": gain(ks[25], (L, MOBA_HEAD_DIM)),
        "mlp_w1": nrm(ks[26], (L, D, D_FF), D ** -0.5),
        "mlp_w2": nrm(ks[27], (L, D_FF, D), D_FF ** -0.5),
        "final_g": gain(ks[28], (D,)),
    }


def reference(x, c, positions, norm1_g, norm2_g, w_ada, b_ada, w_in, w_out,
              ssm_a_re, ssm_a_im, ssm_log_dt, ssm_b_re, ssm_b_im, ssm_c_re, ssm_c_im,
              ssm_d, ssm_glu_w, ssm_glu_b, ssm_norm_g,
              diff_lq1, diff_lk1, diff_lq2, diff_lk2, diff_subln_g, moba_norm_g,
              mlp_w1, mlp_w2, final_g):
    bsz, s, _ = x.shape
    silu_c = jax.nn.silu(c)
    cos_d, sin_d = rope_tables(positions, DIFF_HEAD_DIM)
    cos_m, sin_m = rope_tables(positions, MOBA_HEAD_DIM)
    splits = [SSM_WIDTH,
              SSM_WIDTH + DIFF_WIDTH,
              SSM_WIDTH + 2 * DIFF_WIDTH,
              SSM_WIDTH + 3 * DIFF_WIDTH,
              SSM_WIDTH + 3 * DIFF_WIDTH + MOBA_WIDTH,
              SSM_WIDTH + 3 * DIFF_WIDTH + 2 * MOBA_WIDTH]
    for l in range(DEPTH):
        mod = silu_c @ w_ada[l] + b_ada[l]
        sh1, sc1, g1, sh2, sc2, g2 = jnp.split(mod[:, None, :], 6, axis=-1)

        h = rmsnorm(x, norm1_g[l]) * (1.0 + sc1) + sh1
        proj = h @ w_in[l]
        u, dq, dk, dv, mq, mk, mv = jnp.split(proj, splits, axis=-1)

        y_ssm = s5_mixer(u, ssm_a_re[l], ssm_a_im[l], ssm_log_dt[l], ssm_b_re[l], ssm_b_im[l],
                         ssm_c_re[l], ssm_c_im[l], ssm_d[l], ssm_glu_w[l], ssm_glu_b[l], ssm_norm_g[l])

        lam_init = 0.8 - 0.6 * math.exp(-0.3 * l)
        lam = (jnp.exp(jnp.sum(diff_lq1[l].astype(jnp.float32) * diff_lk1[l].astype(jnp.float32)))
               - jnp.exp(jnp.sum(diff_lq2[l].astype(jnp.float32) * diff_lk2[l].astype(jnp.float32)))
               + lam_init)
        rd = (cos_d[:, :, None, None, :], sin_d[:, :, None, None, :])
        dq = apply_partial_rope(dq.reshape(bsz, s, DIFF_HEADS, 2, DIFF_HEAD_DIM), *rd)
        dk = apply_partial_rope(dk.reshape(bsz, s, DIFF_HEADS, 2, DIFF_HEAD_DIM), *rd)
        dv = dv.reshape(bsz, s, DIFF_HEADS, DIFF_V_DIM)
        y_diff = diff_attention(dq, dk, dv, lam, lam_init, diff_subln_g[l])

        rm = (cos_m[:, :, None, :], sin_m[:, :, None, :])
        mq = apply_partial_rope(mq.reshape(bsz, s, MOBA_HEADS, MOBA_HEAD_DIM), *rm)
        mk = apply_partial_rope(mk.reshape(bsz, s, MOBA_HEADS, MOBA_HEAD_DIM), *rm)
        mv = mv.reshape(bsz, s, MOBA_HEADS, MOBA_HEAD_DIM)
        y_moba = rmsnorm(moba_attention(mq, mk, mv), moba_norm_g[l]).reshape(bsz, s, MOBA_WIDTH)

        mix = jnp.concatenate([y_ssm, y_diff, y_moba], axis=-1)
        x = x + g1 * (mix @ w_out[l])

        h = rmsnorm(x, norm2_g[l]) * (1.0 + sc2) + sh2
        x = x + g2 * (jnp.square(jax.nn.relu(h @ mlp_w1[l])) @ mlp_w2[l])
    return rmsnorm(x, final_g)
```

```python
import functools
import math

import jax
import jax.numpy as jnp
from jax import lax
from jax.experimental import pallas as pl
from jax.experimental.pallas import tpu as pltpu

F32 = jnp.float32
BF16 = jnp.bfloat16

SSM_GROUP = 16
SSM_STATE = 64
HEAD_DIM = 64
DIFF_HEADS = 4
MOBA_HEADS = 4
MOBA_BLOCK = 256
MOBA_TOPK = 3
ROPE_THETA = 500000.0
ROPE_HALF = 8
EPS = 1e-6
NEG = -1e30
BELOW_NEG = -3e38

SUBLANES = 8
TOKEN_TILE = 512
VMEM_LIMIT = 56 * 1024 * 1024


def _params(*sem):
    return pltpu.CompilerParams(dimension_semantics=sem, vmem_limit_bytes=VMEM_LIMIT)


def _const_spec(shape):
    nd = len(shape)
    return pl.BlockSpec(shape, lambda *_: (0,) * nd)


def _rope_kernel(inv_ref, pos_ref, cos_ref, sin_ref):
    ang = inv_ref[...] * pos_ref[0].astype(F32)
    cos_ref[0] = jnp.cos(ang)
    sin_ref[0] = jnp.sin(ang)


def _rope_tables(positions):
    bsz, s = positions.shape
    rot = 2 * ROPE_HALF
    inv = (ROPE_THETA ** (-jnp.arange(0, rot, 2, dtype=F32) / rot)).reshape(ROPE_HALF, 1)
    out = jax.ShapeDtypeStruct((bsz, ROPE_HALF, s), F32)
    return pl.pallas_call(
        _rope_kernel,
        grid=(bsz,),
        in_specs=[_const_spec((ROPE_HALF, 1)),
                  pl.BlockSpec((1, 1, s), lambda b: (b, 0, 0))],
        out_specs=[pl.BlockSpec((1, ROPE_HALF, s), lambda b: (b, 0, 0))] * 2,
        out_shape=[out, out],
        compiler_params=_params("parallel"),
        name="rope_tables",
    )(inv, positions.reshape(bsz, 1, s))


def _ada_kernel(c_ref, w_ref, b_ref, o_ref):
    c = c_ref[...]
    sc = c * jax.nn.sigmoid(c)
    o_ref[0] = jnp.dot(sc, w_ref[0], preferred_element_type=F32,
                       precision=lax.Precision.HIGHEST) + b_ref[0]


def _ada_mod(c, w_ada, b_ada):
    depth, d, d6 = w_ada.shape
    bsz = c.shape[0]
    nchunk = d6 // d
    return pl.pallas_call(
        _ada_kernel,
        grid=(depth, nchunk),
        in_specs=[_const_spec((bsz, d)),
                  pl.BlockSpec((1, d, d), lambda l, j: (l, 0, j)),
                  pl.BlockSpec((1, 1, d), lambda l, j: (l, 0, j))],
        out_specs=pl.BlockSpec((1, bsz, d), lambda l, j: (l, 0, j)),
        out_shape=jax.ShapeDtypeStruct((depth, bsz, d6), F32),
        compiler_params=_params("parallel", "parallel"),
        name="ada_mod",
    )(c, w_ada, b_ada.reshape(depth, 1, d6))


def _ssm_prep_kernel(are_ref, aim_ref, ldt_ref, bre_ref, bim_ref,
                     bbr_ref, bbi_ref, pwr_ref, pwi_ref):
    ar = are_ref[...]
    ai = aim_ref[...]
    dt = jnp.exp(ldt_ref[...])
    mag = jnp.exp(ar * dt)
    abr = mag * jnp.cos(ai * dt)
    abi = mag * jnp.sin(ai * dt)
    den = ar * ar + ai * ai
    nr = abr - 1.0
    ni = abi
    kr = (nr * ar + ni * ai) / den
    ki = (ni * ar - nr * ai) / den
    br = bre_ref[...]
    bi = bim_ref[...]
    bbr_ref[...] = kr[None] * br - ki[None] * bi
    bbi_ref[...] = kr[None] * bi + ki[None] * br
    pr, pi = abr, abi
    pwr_ref[0] = pr
    pwi_ref[0] = pi
    for j in range(1, SUBLANES):
        pr, pi = pr * abr - pi * abi, pr * abi + pi * abr
        pwr_ref[j] = pr
        pwi_ref[j] = pi


def _ssm_prep(a_re, a_im, log_dt, b_re, b_im, c_re, c_im):
    g, p = a_re.shape
    cch = b_re.shape[-1]
    w = g * cch
    n = g * p
    bt = lambda b: jnp.transpose(b, (2, 0, 1))
    shapes = [jax.ShapeDtypeStruct((cch, g, p), F32)] * 2 + [jax.ShapeDtypeStruct((SUBLANES, g, p), F32)] * 2
    bbr, bbi, pwr, pwi = pl.pallas_call(
        _ssm_prep_kernel, out_shape=shapes, name="ssm_prep",
    )(a_re, a_im, log_dt.reshape(g, 1), bt(b_re), bt(b_im))
    eye = jnp.eye(g, dtype=F32)
    blk_b = lambda bb: jnp.einsum('cgp,gh->gchp', bb, eye).reshape(w, n)
    b_blk = jnp.concatenate([blk_b(bbr), blk_b(bbi)], axis=1).astype(BF16)
    blk_c = lambda cc: jnp.einsum('gcp,gh->gphc', cc, eye).reshape(n, w)
    c_blk = jnp.concatenate([blk_c(c_re), -blk_c(c_im)], axis=0).astype(BF16)
    pwr = pwr.reshape(SUBLANES, n)
    pwi = pwi.reshape(SUBLANES, n)
    row = jnp.arange(SUBLANES)[:, None]
    tabs = []
    for shift in (1, 2, 4):
        keep = row >= shift
        tabs.append(jnp.where(keep, pwr[shift - 1][None], 0.0))
        tabs.append(jnp.where(keep, pwi[shift - 1][None], 0.0))
    tabs += [pwr, pwi]
    return b_blk, c_blk, jnp.stack(tabs)


def _rmsnorm_rows(x, g):
    return x * lax.rsqrt(jnp.mean(x * x, axis=-1, keepdims=True) + EPS) * g


def _proj_kernel(x_ref, mod_ref, g_ref, wu_ref, wt_ref, cos_ref, sin_ref,
                 u_ref, dq_ref, dk_ref, dv_ref, mq_ref, mk_ref, mv_ref, *, n_rot, widths):
    wdq, wdk, wmq, wmk, wdv, wmv = widths
    x = x_ref[0]
    mod = mod_ref[0]
    h = _rmsnorm_rows(x, g_ref[...]) * (1.0 + mod[1:2]) + mod[0:1]
    hb = h.astype(BF16)
    u_ref[0] = jnp.dot(hb, wu_ref[...], preferred_element_type=F32)
    pt = lax.dot_general(wt_ref[...], hb, (((1,), (1,)), ((), ())), preferred_element_type=F32)
    tm = pt.shape[1]
    rot = pt[:n_rot].reshape(n_rot // HEAD_DIM, HEAD_DIM, tm)
    cos = cos_ref[0][None]
    sin = sin_ref[0][None]
    x1 = rot[:, 0:ROPE_HALF]
    x2 = rot[:, ROPE_HALF:2 * ROPE_HALF]
    rot = jnp.concatenate([x1 * cos - x2 * sin, x2 * cos + x1 * sin, rot[:, 2 * ROPE_HALF:]], axis=1)
    rot = rot.reshape(n_rot, tm)
    qscale = HEAD_DIM ** -0.5
    o = 0
    dq_ref[0, 0] = (rot[o:o + wdq] * qscale).astype(BF16); o += wdq
    dk_ref[0] = rot[o:o + wdk].T.astype(BF16); o += wdk
    mq = (rot[o:o + wmq] * qscale).astype(BF16); o += wmq
    mk_ref[0] = rot[o:o + wmk].T.astype(BF16); o += wmk
    dv_ref[0, 0] = pt[o:o + wdv].astype(BF16); o += wdv
    mv = pt[o:o + wmv].astype(BF16)
    for j in range(tm // MOBA_BLOCK):
        mq_ref[0, j] = mq[:, j * MOBA_BLOCK:(j + 1) * MOBA_BLOCK]
        mv_ref[0, j] = mv[:, j * MOBA_BLOCK:(j + 1) * MOBA_BLOCK]


def _projection(x, mod, g, wu, wt, cos_t, sin_t, widths):
    bsz, s, d = x.shape
    tm = TOKEN_TILE
    ns = s // tm
    wdq, wdk, wmq, wmk, wdv, wmv = widths
    n_rot = wdq + wdk + wmq + wmk
    nmb = tm // MOBA_BLOCK
    kern = functools.partial(_proj_kernel, n_rot=n_rot, widths=widths)
    out_shape = [
        jax.ShapeDtypeStruct((bsz, s, wu.shape[1]), F32),
        jax.ShapeDtypeStruct((bsz, ns, wdq, tm), BF16),
        jax.ShapeDtypeStruct((bsz, s, wdk), BF16),
        jax.ShapeDtypeStruct((bsz, ns, wdv, tm), BF16),
        jax.ShapeDtypeStruct((bsz, s // MOBA_BLOCK, wmq, MOBA_BLOCK), BF16),
        jax.ShapeDtypeStruct((bsz, s, wmk), BF16),
        jax.ShapeDtypeStruct((bsz, s // MOBA_BLOCK, wmv, MOBA_BLOCK), BF16),
    ]
    out_specs = [
        pl.BlockSpec((1, tm, wu.shape[1]), lambda b, i: (b, i, 0)),
        pl.BlockSpec((1, 1, wdq, tm), lambda b, i: (b, i, 0, 0)),
        pl.BlockSpec((1, tm, wdk), lambda b, i: (b, i, 0)),
        pl.BlockSpec((1, 1, wdv, tm), lambda b, i: (b, i, 0, 0)),
        pl.BlockSpec((1, nmb, wmq, MOBA_BLOCK), lambda b, i: (b, i, 0, 0)),
        pl.BlockSpec((1, tm, wmk), lambda b, i: (b, i, 0)),
        pl.BlockSpec((1, nmb, wmv, MOBA_BLOCK), lambda b, i: (b, i, 0, 0)),
    ]
    in_specs = [
        pl.BlockSpec((1, tm, d), lambda b, i: (b, i, 0)),
        pl.BlockSpec((1, 6, d), lambda b, i: (b, 0, 0)),
        _const_spec((1, d)),
        _const_spec(wu.shape),
        _const_spec(wt.shape),
        pl.BlockSpec((1, ROPE_HALF, tm), lambda b, i: (b, 0, i)),
        pl.BlockSpec((1, ROPE_HALF, tm), lambda b, i: (b, 0, i)),
    ]
    return pl.pallas_call(
        kern, grid=(bsz, ns), in_specs=in_specs, out_specs=out_specs, out_shape=out_shape,
        compiler_params=_params("parallel", "parallel"), name="projection",
    )(x, mod, g, wu, wt, cos_t, sin_t)


def _gelu_tanh(y):
    return 0.5 * y * (1.0 + jnp.tanh(math.sqrt(2.0 / math.pi) * (y + 0.044715 * (y * y * y))))


def _ssm_kernel(u_ref, bblk_ref, cblk_ref, coef_ref, d_ref, gw_ref, gb_ref, ng_ref,
                y_ref, bu_ref, carry_ref, *, n_state):
    n = n_state

    @pl.when(pl.program_id(1) == 0)
    def _():
        carry_ref[...] = jnp.zeros_like(carry_ref)

    u = u_ref[0]
    tm = u.shape[0]
    bu_ref[...] = jnp.dot(u.astype(BF16), bblk_ref[...], preferred_element_type=F32)

    def cmul_add(xr, xi, ar, ai, sr, si):
        return xr + (ar * sr - ai * si), xi + (ar * si + ai * sr)

    def group(r, carry):
        cr, ci = carry
        rows = pl.ds(pl.multiple_of(r * SUBLANES, SUBLANES), SUBLANES)
        xr = bu_ref[rows, 0:n]
        xi = bu_ref[rows, n:2 * n]
        for lvl, shift in enumerate((1, 2, 4)):
            sr = pltpu.roll(xr, shift, axis=0)
            si = pltpu.roll(xi, shift, axis=0)
            xr, xi = cmul_add(xr, xi, coef_ref[2 * lvl], coef_ref[2 * lvl + 1], sr, si)
        xr, xi = cmul_add(xr, xi, coef_ref[6], coef_ref[7], cr, ci)
        bu_ref[rows, 0:n] = xr
        bu_ref[rows, n:2 * n] = xi
        last = SUBLANES - 1
        return (jnp.broadcast_to(xr[last:last + 1], xr.shape),
                jnp.broadcast_to(xi[last:last + 1], xi.shape))

    cr, ci = lax.fori_loop(0, tm // SUBLANES, group, (carry_ref[0], carry_ref[1]))
    carry_ref[0] = cr
    carry_ref[1] = ci

    y = jnp.dot(bu_ref[...].astype(BF16), cblk_ref[...], preferred_element_type=F32)
    y = y + d_ref[...] * u
    y = _gelu_tanh(y)
    z = jnp.dot(y.astype(BF16), gw_ref[...], preferred_element_type=F32) + gb_ref[...]
    y = y * jax.nn.sigmoid(z)
    y_ref[0] = _rmsnorm_rows(y, ng_ref[...]).astype(y_ref.dtype)


def _ssm(u, b_blk, c_blk, coef, d_skip, glu_w, glu_b, norm_g):
    bsz, s, w = u.shape
    n = coef.shape[-1]
    tm = TOKEN_TILE
    kern = functools.partial(_ssm_kernel, n_state=n)
    in_specs = [
        pl.BlockSpec((1, tm, w), lambda b, i: (b, i, 0)),
        _const_spec(b_blk.shape), _const_spec(c_blk.shape), _const_spec(coef.shape),
        _const_spec((1, w)), _const_spec((w, w)), _const_spec((1, w)), _const_spec((1, w)),
    ]
    return pl.pallas_call(
        kern, grid=(bsz, s // tm), in_specs=in_specs,
        out_specs=pl.BlockSpec((1, tm, w), lambda b, i: (b, i, 0)),
        out_shape=jax.ShapeDtypeStruct((bsz, s, w), BF16),
        scratch_shapes=[pltpu.VMEM((tm, 2 * n), F32), pltpu.VMEM((2, SUBLANES, n), F32)],
        compiler_params=_params("parallel", "arbitrary"), name="ssm",
    )(u, b_blk, c_blk, coef, d_skip, glu_w, glu_b, norm_g)


def _softmax_first(s, vt):
    m = jnp.max(s, axis=0, keepdims=True)
    p = jnp.exp(s - m)
    l = jnp.sum(p, axis=0, keepdims=True)
    acc = jnp.dot(vt, p.astype(BF16), preferred_element_type=F32)
    return m, l, acc


def _softmax_next(s, vt, m, l, acc):
    m_new = jnp.maximum(m, jnp.max(s, axis=0, keepdims=True))
    alpha = jnp.exp(m - m_new)
    p = jnp.exp(s - m_new)
    l = alpha * l + jnp.sum(p, axis=0, keepdims=True)
    acc = alpha * acc + jnp.dot(vt, p.astype(BF16), preferred_element_type=F32)
    return m_new, l, acc


def _causal_mask(t):
    key = lax.broadcasted_iota(jnp.int32, (t, t), 0)
    qry = lax.broadcasted_iota(jnp.int32, (t, t), 1)
    return key <= qry


def _diff_kernel(lam_ref, g_ref, qt_ref, k_ref, vt_ref, o_ref, acc1_ref, acc2_ref, *, lam_init):
    qi = pl.program_id(2)
    qt = qt_ref[0, 0].astype(F32)
    t = qt.shape[1]
    row = lax.broadcasted_iota(jnp.int32, qt.shape, 0)
    q1 = jnp.where(row < HEAD_DIM, qt, 0.0).astype(BF16)
    q2 = jnp.where(row >= HEAD_DIM, qt, 0.0).astype(BF16)

    def scores(kt):
        kb = k_ref[0, pl.ds(pl.multiple_of(kt * t, t), t), :]
        return (jnp.dot(kb, q1, preferred_element_type=F32),
                jnp.dot(kb, q2, preferred_element_type=F32))

    s1, s2 = scores(qi)
    mask = _causal_mask(t)
    vt = vt_ref[0, qi]
    m1, l1, a1 = _softmax_first(jnp.where(mask, s1, NEG), vt)
    m2, l2, a2 = _softmax_first(jnp.where(mask, s2, NEG), vt)
    acc1_ref[...] = a1
    acc2_ref[...] = a2

    def step(kt, carry):
        m1, l1, m2, l2 = carry
        s1, s2 = scores(kt)
        vt = vt_ref[0, kt]
        m1, l1, a1 = _softmax_next(s1, vt, m1, l1, acc1_ref[...])
        acc1_ref[...] = a1
        m2, l2, a2 = _softmax_next(s2, vt, m2, l2, acc2_ref[...])
        acc2_ref[...] = a2
        return m1, l1, m2, l2

    m1, l1, m2, l2 = lax.fori_loop(0, qi, step, (m1, l1, m2, l2))

    lp = lam_ref[...]
    lam = (jnp.exp(jnp.sum(lp[0:1] * lp[1:2], axis=1, keepdims=True))
           - jnp.exp(jnp.sum(lp[2:3] * lp[3:4], axis=1, keepdims=True)) + lam_init)
    o = acc1_ref[...] / l1 - lam * (acc2_ref[...] / l2)
    o = o * lax.rsqrt(jnp.mean(o * o, axis=0, keepdims=True) + EPS) * g_ref[...]
    o_ref[0] = (o * (1.0 - lam_init)).T.astype(o_ref.dtype)


def _diff_attention(qt, k, vt, lam_params, subln_g, lam_init):
    bsz, ns, wq, t = qt.shape
    s = k.shape[1]
    dv = 2 * HEAD_DIM
    nh = wq // dv
    kern = functools.partial(_diff_kernel, lam_init=lam_init)
    in_specs = [
        _const_spec(lam_params.shape),
        _const_spec((dv, 1)),
        pl.BlockSpec((1, 1, dv, t), lambda b, h, i: (b, i, h, 0)),
        pl.BlockSpec((1, s, dv), lambda b, h, i: (b, 0, h)),
        pl.BlockSpec((1, ns, dv, t), lambda b, h, i: (b, 0, h, 0)),
    ]
    return pl.pallas_call(
        kern, grid=(bsz, nh, ns), in_specs=in_specs,
        out_specs=pl.BlockSpec((1, t, dv), lambda b, h, i: (b, i, h)),
        out_shape=jax.ShapeDtypeStruct((bsz, s, wq), BF16),
        scratch_shapes=[pltpu.VMEM((dv, t), F32), pltpu.VMEM((dv, t), F32)],
        compiler_params=_params("parallel", "parallel", "arbitrary"), name="diff_attention",
    )(lam_params, subln_g.reshape(dv, 1), qt, k, vt)


def _moba_kernel(g_ref, qt_ref, k_ref, vt_ref, o_ref, kmean_ref, sel_ref, acc_ref, *, n_blocks):
    qi = pl.program_id(2)
    blk = MOBA_BLOCK
    pair = 2 * HEAD_DIM

    @pl.when(qi == 0)
    def _():
        def mean_block(j, c):
            kb = k_ref[0, pl.ds(pl.multiple_of(j * blk, blk), blk), :].astype(F32)
            kmean_ref[pl.ds(j, 1), :] = jnp.mean(kb, axis=0, keepdims=True)
            return c
        lax.fori_loop(0, n_blocks, mean_block, 0)

    qt = qt_ref[0, 0].astype(F32)
    row = lax.broadcasted_iota(jnp.int32, qt.shape, 0)
    lane = lax.broadcasted_iota(jnp.int32, (n_blocks, pair), 1)
    bid = lax.broadcasted_iota(jnp.int32, (n_blocks, blk), 0).astype(F32)
    past = bid < qi.astype(F32)
    kmean = kmean_ref[...]
    mask = _causal_mask(blk)
    outs = []
    for hh in range(2):
        in_head = (row >= hh * HEAD_DIM) & (row < (hh + 1) * HEAD_DIM)
        qf = jnp.where(in_head, qt, 0.0)
        qh = qf.astype(BF16)
        km = jnp.where((lane >= hh * HEAD_DIM) & (lane < (hh + 1) * HEAD_DIM), kmean, 0.0)
        gate = jnp.dot(km, qf, preferred_element_type=F32, precision=lax.Precision.HIGHEST)
        gate = jnp.where(past, gate, NEG)
        sel = jnp.zeros_like(gate)
        for _ in range(MOBA_TOPK):
            top = jnp.max(gate, axis=0, keepdims=True)
            first = jnp.min(jnp.where(gate == top, bid, float(n_blocks)), axis=0, keepdims=True)
            hit = bid == first
            sel = jnp.where(hit, 1.0, sel)
            gate = jnp.where(hit, BELOW_NEG, gate)
        sel_ref[hh] = jnp.where(past, sel, 0.0)

        def scores(j):
            kb = k_ref[0, pl.ds(pl.multiple_of(j * blk, blk), blk), :]
            return jnp.dot(kb, qh, preferred_element_type=F32)

        def values(j):
            return vt_ref[0, j, hh * HEAD_DIM:(hh + 1) * HEAD_DIM, :]

        m, l, a = _softmax_first(jnp.where(mask, scores(qi), NEG), values(qi))
        acc_ref[hh] = a

        def step(j, carry, hh=hh, scores=scores, values=values):
            m, l = carry
            chosen = sel_ref[hh, pl.ds(j, 1), :] > 0.0
            s = jnp.where(chosen, scores(j), NEG)
            m, l, a = _softmax_next(s, values(j), m, l, acc_ref[hh])
            acc_ref[hh] = a
            return m, l

        m, l = lax.fori_loop(0, qi, step, (m, l))
        o = acc_ref[hh] / l
        outs.append(o * lax.rsqrt(jnp.mean(o * o, axis=0, keepdims=True) + EPS) * g_ref[...])
    o_ref[0] = jnp.concatenate(outs, axis=0).T.astype(o_ref.dtype)


def _moba_attention(qt, k, vt, norm_g):
    bsz, nb, wq, blk = qt.shape
    s = k.shape[1]
    pair = 2 * HEAD_DIM
    kern = functools.partial(_moba_kernel, n_blocks=nb)
    in_specs = [
        _const_spec((HEAD_DIM, 1)),
        pl.BlockSpec((1, 1, pair, blk), lambda b, h, i: (b, i, h, 0)),
        pl.BlockSpec((1, s, pair), lambda b, h, i: (b, 0, h)),
        pl.BlockSpec((1, nb, pair, blk), lambda b, h, i: (b, 0, h, 0)),
    ]
    return pl.pallas_call(
        kern, grid=(bsz, wq // pair, nb), in_specs=in_specs,
        out_specs=pl.BlockSpec((1, blk, pair), lambda b, h, i: (b, i, h)),
        out_shape=jax.ShapeDtypeStruct((bsz, s, wq), BF16),
        scratch_shapes=[pltpu.VMEM((nb, pair), F32), pltpu.VMEM((2, nb, blk), F32),
                        pltpu.VMEM((2, HEAD_DIM, blk), F32)],
        compiler_params=_params("parallel", "parallel", "arbitrary"), name="moba_attention",
    )(norm_g.reshape(HEAD_DIM, 1), qt, k, vt)


def _out_mlp_kernel(x_ref, ys_ref, yd_ref, ym_ref, mod_ref, g2_ref, wo_ref, w1_ref, w2_ref, fg_ref,
                    o_ref, *, final_norm, ff_chunk):
    x = x_ref[0]
    mod = mod_ref[0]
    ws = ys_ref.shape[2]
    wd = yd_ref.shape[2]
    mix = (jnp.dot(ys_ref[0], wo_ref[0:ws], preferred_element_type=F32)
           + jnp.dot(yd_ref[0], wo_ref[ws:ws + wd], preferred_element_type=F32)
           + jnp.dot(ym_ref[0], wo_ref[ws + wd:], preferred_element_type=F32))
    x = x + mod[2:3] * mix
    h = (_rmsnorm_rows(x, g2_ref[...]) * (1.0 + mod[4:5]) + mod[3:4]).astype(BF16)
    acc = jnp.zeros_like(x)
    for c in range(w1_ref.shape[1] // ff_chunk):
        a = jnp.dot(h, w1_ref[:, c * ff_chunk:(c + 1) * ff_chunk], preferred_element_type=F32)
        a = jnp.square(jnp.maximum(a, 0.0)).astype(BF16)
        acc = acc + jnp.dot(a, w2_ref[c * ff_chunk:(c + 1) * ff_chunk, :], preferred_element_type=F32)
    x = x + mod[5:6] * acc
    if final_norm:
        x = _rmsnorm_rows(x, fg_ref[...])
    o_ref[0] = x


def _out_mlp(x, ys, yd, ym, mod, g2, wo, w1, w2, final_g, final_norm):
    bsz, s, d = x.shape
    tm = TOKEN_TILE
    kern = functools.partial(_out_mlp_kernel, final_norm=final_norm, ff_chunk=d)
    tok = lambda w: pl.BlockSpec((1, tm, w), lambda b, i: (b, i, 0))
    in_specs = [
        tok(d), tok(ys.shape[2]), tok(yd.shape[2]), tok(ym.shape[2]),
        pl.BlockSpec((1, 6, d), lambda b, i: (b, 0, 0)),
        _const_spec((1, d)), _const_spec(wo.shape), _const_spec(w1.shape), _const_spec(w2.shape),
        _const_spec((1, d)),
    ]
    return pl.pallas_call(
        kern, grid=(bsz, s // tm), in_specs=in_specs, out_specs=tok(d),
        out_shape=jax.ShapeDtypeStruct((bsz, s, d), F32),
        compiler_params=_params("parallel", "parallel"), name="out_mlp",
    )(x, ys, yd, ym, mod, g2, wo, w1, w2, final_g)


def kernel(x, c, positions, norm1_g, norm2_g, w_ada, b_ada, w_in, w_out, ssm_a_re, ssm_a_im, ssm_log_dt, ssm_b_re, ssm_b_im, ssm_c_re, ssm_c_im, ssm_d, ssm_glu_w, ssm_glu_b, ssm_norm_g, diff_lq1, diff_lk1, diff_lq2, diff_lk2, diff_subln_g, moba_norm_g, mlp_w1, mlp_w2, final_g):
    bsz, s, d = x.shape
    depth = w_in.shape[0]
    w_ssm = ssm_d.shape[1]
    w_diff = DIFF_HEADS * 2 * HEAD_DIM
    w_moba = MOBA_HEADS * HEAD_DIM
    assert s % TOKEN_TILE == 0 and TOKEN_TILE % MOBA_BLOCK == 0
    assert w_in.shape[2] == w_ssm + 3 * w_diff + 3 * w_moba

    cos_t, sin_t = _rope_tables(positions)
    mod_all = _ada_mod(c, w_ada, b_ada).reshape(depth, bsz, 6, d)

    o_dq = w_ssm
    o_dk = o_dq + w_diff
    o_dv = o_dk + w_diff
    o_mq = o_dv + w_diff
    o_mk = o_mq + w_moba
    o_mv = o_mk + w_moba
    widths = (w_diff, w_diff, w_moba, w_moba, w_diff, w_moba)

    for l in range(depth):
        wl = w_in[l]
        wu = wl[:, :o_dq].astype(BF16)
        wt = jnp.concatenate([wl[:, o_dq:o_dk], wl[:, o_dk:o_dv], wl[:, o_mq:o_mk], wl[:, o_mk:o_mv],
                              wl[:, o_dv:o_mq], wl[:, o_mv:]], axis=1).T.astype(BF16)
        mod = mod_all[l]
        u, dq, dk, dv, mq, mk, mv = _projection(x, mod, norm1_g[l][None], wu, wt, cos_t, sin_t, widths)

        b_blk, c_blk, coef = _ssm_prep(ssm_a_re[l], ssm_a_im[l], ssm_log_dt[l], ssm_b_re[l], ssm_b_im[l],
                                       ssm_c_re[l], ssm_c_im[l])
        y_ssm = _ssm(u, b_blk, c_blk, coef, ssm_d[l][None], ssm_glu_w[l].astype(BF16),
                     ssm_glu_b[l][None], ssm_norm_g[l][None])

        lam_init = 0.8 - 0.6 * math.exp(-0.3 * l)
        lam_params = jnp.stack([diff_lq1[l], diff_lk1[l], diff_lq2[l], diff_lk2[l]]).astype(F32)
        y_diff = _diff_attention(dq, dk, dv, lam_params, diff_subln_g[l], lam_init)
        y_moba = _moba_attention(mq, mk, mv, moba_norm_g[l])

        x = _out_mlp(x, y_ssm, y_diff, y_moba, mod, norm2_g[l][None], w_out[l].astype(BF16),
                     mlp_w1[l].astype(BF16), mlp_w2[l].astype(BF16), final_g[None],
                     final_norm=(l == depth - 1))
    return x
```

```python
import functools
import math

import jax
import jax.numpy as jnp
from jax import lax
from jax.experimental import pallas as pl
from jax.experimental.pallas import tpu as pltpu

F32 = jnp.float32
BF16 = jnp.bfloat16

SSM_GROUP = 16
SSM_STATE = 64
HEAD_DIM = 64
DIFF_HEADS = 4
MOBA_HEADS = 4
MOBA_BLOCK = 256
MOBA_TOPK = 3
ROPE_THETA = 500000.0
ROPE_HALF = 8
EPS = 1e-6
NEG = -1e30
BELOW_NEG = -3e38

SUBLANES = 8
TOKEN_TILE = 512
ATTN_BLOCK = MOBA_BLOCK
ATTN_CHUNK = 4
VMEM_LIMIT = 56 * 1024 * 1024


def _params(*sem):
    return pltpu.CompilerParams(dimension_semantics=sem, vmem_limit_bytes=VMEM_LIMIT)


def _const_spec(shape):
    nd = len(shape)
    return pl.BlockSpec(shape, lambda *_: (0,) * nd)


def _rope_kernel(inv_ref, pos_ref, cos_ref, sin_ref):
    ang = inv_ref[...] * pos_ref[0].astype(F32)
    cos_ref[0] = jnp.cos(ang)
    sin_ref[0] = jnp.sin(ang)


def _rope_tables(positions):
    bsz, s = positions.shape
    rot = 2 * ROPE_HALF
    inv = (ROPE_THETA ** (-jnp.arange(0, rot, 2, dtype=F32) / rot)).reshape(ROPE_HALF, 1)
    out = jax.ShapeDtypeStruct((bsz, ROPE_HALF, s), F32)
    return pl.pallas_call(
        _rope_kernel,
        grid=(bsz,),
        in_specs=[_const_spec((ROPE_HALF, 1)),
                  pl.BlockSpec((1, 1, s), lambda b: (b, 0, 0))],
        out_specs=[pl.BlockSpec((1, ROPE_HALF, s), lambda b: (b, 0, 0))] * 2,
        out_shape=[out, out],
        compiler_params=_params("parallel"),
        name="rope_tables",
    )(inv, positions.reshape(bsz, 1, s))


def _ada_kernel(c_ref, w_ref, b_ref, o_ref):
    c = c_ref[...]
    sc = c * jax.nn.sigmoid(c)
    o_ref[0] = jnp.dot(sc, w_ref[0], preferred_element_type=F32,
                       precision=lax.Precision.HIGHEST) + b_ref[0]


def _ada_mod(c, w_ada, b_ada):
    depth, d, d6 = w_ada.shape
    bsz = c.shape[0]
    nchunk = d6 // d
    return pl.pallas_call(
        _ada_kernel,
        grid=(depth, nchunk),
        in_specs=[_const_spec((bsz, d)),
                  pl.BlockSpec((1, d, d), lambda l, j: (l, 0, j)),
                  pl.BlockSpec((1, 1, d), lambda l, j: (l, 0, j))],
        out_specs=pl.BlockSpec((1, bsz, d), lambda l, j: (l, 0, j)),
        out_shape=jax.ShapeDtypeStruct((depth, bsz, d6), F32),
        compiler_params=_params("parallel", "parallel"),
        name="ada_mod",
    )(c, w_ada, b_ada.reshape(depth, 1, d6))


def _ssm_prep_kernel(are_ref, aim_ref, ldt_ref, bre_ref, bim_ref,
                     bbr_ref, bbi_ref, pwr_ref, pwi_ref):
    ar = are_ref[...]
    ai = aim_ref[...]
    dt = jnp.exp(ldt_ref[...])
    mag = jnp.exp(ar * dt)
    abr = mag * jnp.cos(ai * dt)
    abi = mag * jnp.sin(ai * dt)
    den = ar * ar + ai * ai
    nr = abr - 1.0
    ni = abi
    kr = (nr * ar + ni * ai) / den
    ki = (ni * ar - nr * ai) / den
    br = bre_ref[...]
    bi = bim_ref[...]
    bbr_ref[...] = kr[None] * br - ki[None] * bi
    bbi_ref[...] = kr[None] * bi + ki[None] * br
    pr, pi = abr, abi
    pwr_ref[0] = pr
    pwi_ref[0] = pi
    for j in range(1, SUBLANES):
        pr, pi = pr * abr - pi * abi, pr * abi + pi * abr
        pwr_ref[j] = pr
        pwi_ref[j] = pi


def _ssm_prep(a_re, a_im, log_dt, b_re, b_im, c_re, c_im):
    g, p = a_re.shape
    cch = b_re.shape[-1]
    w = g * cch
    n = g * p
    bt = lambda b: jnp.transpose(b, (2, 0, 1))
    shapes = [jax.ShapeDtypeStruct((cch, g, p), F32)] * 2 + [jax.ShapeDtypeStruct((SUBLANES, g, p), F32)] * 2
    bbr, bbi, pwr, pwi = pl.pallas_call(
        _ssm_prep_kernel, out_shape=shapes, name="ssm_prep",
    )(a_re, a_im, log_dt.reshape(g, 1), bt(b_re), bt(b_im))
    eye = jnp.eye(g, dtype=F32)
    blk_b = lambda bb: jnp.einsum('cgp,gh->gchp', bb, eye).reshape(w, n)
    b_blk = jnp.concatenate([blk_b(bbr), blk_b(bbi)], axis=1).astype(BF16)
    blk_c = lambda cc: jnp.einsum('gcp,gh->gphc', cc, eye).reshape(n, w)
    c_blk = jnp.concatenate([blk_c(c_re), -blk_c(c_im)], axis=0).astype(BF16)
    pwr = pwr.reshape(SUBLANES, n)
    pwi = pwi.reshape(SUBLANES, n)
    row = jnp.arange(SUBLANES)[:, None]
    tabs = []
    for shift in (1, 2, 4):
        keep = row >= shift
        tabs.append(jnp.where(keep, pwr[shift - 1][None], 0.0))
        tabs.append(jnp.where(keep, pwi[shift - 1][None], 0.0))
    tabs += [pwr, pwi]
    return b_blk, c_blk, jnp.stack(tabs)


def _rmsnorm_rows(x, g):
    return x * lax.rsqrt(jnp.mean(x * x, axis=-1, keepdims=True) + EPS) * g


def _proj_kernel(x_ref, mod_ref, g_ref, wu_ref, wt_ref, cos_ref, sin_ref,
                 u_ref, dq_ref, dk_ref, dv_ref, mq_ref, mk_ref, mv_ref, *, n_rot, widths):
    wdq, wdk, wmq, wmk, wdv, wmv = widths
    x = x_ref[0]
    mod = mod_ref[0]
    h = _rmsnorm_rows(x, g_ref[...]) * (1.0 + mod[1:2]) + mod[0:1]
    hb = h.astype(BF16)
    u_ref[0] = jnp.dot(hb, wu_ref[...], preferred_element_type=F32)
    pt = lax.dot_general(wt_ref[...], hb, (((1,), (1,)), ((), ())), preferred_element_type=F32)
    tm = pt.shape[1]
    rot = pt[:n_rot].reshape(n_rot // HEAD_DIM, HEAD_DIM, tm)
    cos = cos_ref[0][None]
    sin = sin_ref[0][None]
    x1 = rot[:, 0:ROPE_HALF]
    x2 = rot[:, ROPE_HALF:2 * ROPE_HALF]
    rot = jnp.concatenate([x1 * cos - x2 * sin, x2 * cos + x1 * sin, rot[:, 2 * ROPE_HALF:]], axis=1)
    rot = rot.reshape(n_rot, tm)
    qscale = HEAD_DIM ** -0.5
    o = 0
    dq = (rot[o:o + wdq] * qscale).astype(BF16); o += wdq
    dk_ref[0] = rot[o:o + wdk].T.astype(BF16); o += wdk
    mq = (rot[o:o + wmq] * qscale).astype(BF16); o += wmq
    mk_ref[0] = rot[o:o + wmk].T.astype(BF16); o += wmk
    dv = pt[o:o + wdv].astype(BF16); o += wdv
    mv = pt[o:o + wmv].astype(BF16)
    for j in range(tm // ATTN_BLOCK):
        blk = slice(j * ATTN_BLOCK, (j + 1) * ATTN_BLOCK)
        dq_ref[0, j] = dq[:, blk]
        dv_ref[0, j] = dv[:, blk]
        mq_ref[0, j] = mq[:, blk]
        mv_ref[0, j] = mv[:, blk]


def _projection(x, mod, g, wu, wt, cos_t, sin_t, widths):
    bsz, s, d = x.shape
    tm = TOKEN_TILE
    ns = s // tm
    wdq, wdk, wmq, wmk, wdv, wmv = widths
    n_rot = wdq + wdk + wmq + wmk
    ab = ATTN_BLOCK
    nab = tm // ab
    kern = functools.partial(_proj_kernel, n_rot=n_rot, widths=widths)
    tiles_t = lambda w: jax.ShapeDtypeStruct((bsz, s // ab, w, ab), BF16)
    out_shape = [
        jax.ShapeDtypeStruct((bsz, s, wu.shape[1]), F32),
        tiles_t(wdq),
        jax.ShapeDtypeStruct((bsz, s, wdk), BF16),
        tiles_t(wdv),
        tiles_t(wmq),
        jax.ShapeDtypeStruct((bsz, s, wmk), BF16),
        tiles_t(wmv),
    ]
    spec_t = lambda w: pl.BlockSpec((1, nab, w, ab), lambda b, i: (b, i, 0, 0))
    out_specs = [
        pl.BlockSpec((1, tm, wu.shape[1]), lambda b, i: (b, i, 0)),
        spec_t(wdq),
        pl.BlockSpec((1, tm, wdk), lambda b, i: (b, i, 0)),
        spec_t(wdv),
        spec_t(wmq),
        pl.BlockSpec((1, tm, wmk), lambda b, i: (b, i, 0)),
        spec_t(wmv),
    ]
    in_specs = [
        pl.BlockSpec((1, tm, d), lambda b, i: (b, i, 0)),
        pl.BlockSpec((1, 6, d), lambda b, i: (b, 0, 0)),
        _const_spec((1, d)),
        _const_spec(wu.shape),
        _const_spec(wt.shape),
        pl.BlockSpec((1, ROPE_HALF, tm), lambda b, i: (b, 0, i)),
        pl.BlockSpec((1, ROPE_HALF, tm), lambda b, i: (b, 0, i)),
    ]
    return pl.pallas_call(
        kern, grid=(bsz, ns), in_specs=in_specs, out_specs=out_specs, out_shape=out_shape,
        compiler_params=_params("parallel", "parallel"), name="projection",
    )(x, mod, g, wu, wt, cos_t, sin_t)


def _gelu_tanh(y):
    return 0.5 * y * (1.0 + jnp.tanh(math.sqrt(2.0 / math.pi) * (y + 0.044715 * (y * y * y))))


def _ssm_kernel(u_ref, bblk_ref, cblk_ref, coef_ref, d_ref, gw_ref, gb_ref, ng_ref,
                y_ref, bu_ref, carry_ref, *, n_state):
    n = n_state

    @pl.when(pl.program_id(1) == 0)
    def _():
        carry_ref[...] = jnp.zeros_like(carry_ref)

    u = u_ref[0]
    tm = u.shape[0]
    bu_ref[...] = jnp.dot(u.astype(BF16), bblk_ref[...], preferred_element_type=F32)

    def cmul_add(xr, xi, ar, ai, sr, si):
        return xr + (ar * sr - ai * si), xi + (ar * si + ai * sr)

    def group(r, carry):
        cr, ci = carry
        rows = pl.ds(pl.multiple_of(r * SUBLANES, SUBLANES), SUBLANES)
        xr = bu_ref[rows, 0:n]
        xi = bu_ref[rows, n:2 * n]
        for lvl, shift in enumerate((1, 2, 4)):
            sr = pltpu.roll(xr, shift, axis=0)
            si = pltpu.roll(xi, shift, axis=0)
            xr, xi = cmul_add(xr, xi, coef_ref[2 * lvl], coef_ref[2 * lvl + 1], sr, si)
        xr, xi = cmul_add(xr, xi, coef_ref[6], coef_ref[7], cr, ci)
        bu_ref[rows, 0:n] = xr
        bu_ref[rows, n:2 * n] = xi
        last = SUBLANES - 1
        return (jnp.broadcast_to(xr[last:last + 1], xr.shape),
                jnp.broadcast_to(xi[last:last + 1], xi.shape))

    cr, ci = lax.fori_loop(0, tm // SUBLANES, group, (carry_ref[0], carry_ref[1]))
    carry_ref[0] = cr
    carry_ref[1] = ci

    y = jnp.dot(bu_ref[...].astype(BF16), cblk_ref[...], preferred_element_type=F32)
    y = y + d_ref[...] * u
    y = _gelu_tanh(y)
    z = jnp.dot(y.astype(BF16), gw_ref[...], preferred_element_type=F32) + gb_ref[...]
    y = y * jax.nn.sigmoid(z)
    y_ref[0] = _rmsnorm_rows(y, ng_ref[...]).astype(y_ref.dtype)


def _ssm(u, b_blk, c_blk, coef, d_skip, glu_w, glu_b, norm_g):
    bsz, s, w = u.shape
    n = coef.shape[-1]
    tm = TOKEN_TILE
    kern = functools.partial(_ssm_kernel, n_state=n)
    in_specs = [
        pl.BlockSpec((1, tm, w), lambda b, i: (b, i, 0)),
        _const_spec(b_blk.shape), _const_spec(c_blk.shape), _const_spec(coef.shape),
        _const_spec((1, w)), _const_spec((w, w)), _const_spec((1, w)), _const_spec((1, w)),
    ]
    return pl.pallas_call(
        kern, grid=(bsz, s // tm), in_specs=in_specs,
        out_specs=pl.BlockSpec((1, tm, w), lambda b, i: (b, i, 0)),
        out_shape=jax.ShapeDtypeStruct((bsz, s, w), BF16),
        scratch_shapes=[pltpu.VMEM((tm, 2 * n), F32), pltpu.VMEM((2, SUBLANES, n), F32)],
        compiler_params=_params("parallel", "arbitrary"), name="ssm",
    )(u, b_blk, c_blk, coef, d_skip, glu_w, glu_b, norm_g)


def _row_groups(x):
    rows, lanes = x.shape
    return x.reshape(4, rows // (4 * SUBLANES), SUBLANES, lanes)


def _score_stage(k_ref, first_blk, qz, mask_fn, s_ref, mx_ref):
    ab = ATTN_BLOCK
    for w in range(ATTN_CHUNK):
        kb = k_ref[0, pl.ds(pl.multiple_of((first_blk + w) * ab, ab), ab), :]
        s = mask_fn(w, jnp.dot(kb, qz, preferred_element_type=F32))
        s_ref[w * ab:(w + 1) * ab, :] = s
        mx_ref[w] = jnp.max(jnp.max(_row_groups(s), axis=1), axis=0)


def _softmax_stage(s_ref, mx_ref, vt_ref, first_blk, v_rows, m, l, acc_ref):
    ab = ATTN_BLOCK
    m_new = jnp.maximum(m, jnp.max(jnp.max(mx_ref[...], axis=0), axis=0, keepdims=True))
    alpha = jnp.exp(m - m_new)
    psum = None
    pv = None
    for w in range(ATTN_CHUNK):
        p = jnp.exp(s_ref[w * ab:(w + 1) * ab, :] - m_new)
        part = jnp.sum(jnp.sum(_row_groups(p), axis=1), axis=0)
        psum = part if psum is None else psum + part
        d = jnp.dot(vt_ref[0, first_blk + w, v_rows, :], p.astype(BF16), preferred_element_type=F32)
        pv = d if pv is None else pv + d
    acc_ref[...] = alpha * acc_ref[...] + pv
    return m_new, alpha * l + jnp.sum(psum, axis=0, keepdims=True)


def _flash_pipeline(plan, k_ref, vt_ref, qzs, v_rows, window_masks, chunk_masks, s_refs, mx_refs, acc_refs):
    n_full, win = plan
    t = s_refs[0].shape[-1]
    for acc in acc_refs:
        acc[...] = jnp.zeros(acc.shape, F32)
    neg = jnp.full((1, t), NEG, F32)
    zero = jnp.zeros((1, t), F32)

    def score(c, first, mask_fn):
        _score_stage(k_ref, first, qzs[c], mask_fn, s_refs[c], mx_refs[c])

    def softmax(c, first, m, l):
        return _softmax_stage(s_refs[c], mx_refs[c], vt_ref, first, v_rows[c], m, l, acc_refs[c])

    score(0, 0, chunk_masks(0)[0])

    def step(i, carry):
        first = i * ATTN_CHUNK
        nxt = first + ATTN_CHUNK
        score(1, first, chunk_masks(first)[1])
        mx, lx = softmax(0, first, carry[0], carry[1])
        score(0, nxt, chunk_masks(nxt)[0])
        my, ly = softmax(1, first, carry[2], carry[3])
        return mx, lx, my, ly

    mx, lx, my, ly = lax.fori_loop(0, n_full, step, (neg, zero, neg, zero))
    score(0, win, window_masks[0])
    score(1, win, window_masks[1])
    mx, lx = softmax(0, win, mx, lx)
    my, ly = softmax(1, win, my, ly)
    return mx, lx, my, ly


def _score_scratch(t):
    return [pltpu.VMEM((ATTN_CHUNK * ATTN_BLOCK, t), F32), pltpu.VMEM((ATTN_CHUNK * ATTN_BLOCK, t), F32),
            pltpu.VMEM((ATTN_CHUNK, SUBLANES, t), F32), pltpu.VMEM((ATTN_CHUNK, SUBLANES, t), F32)]


def _key_minus_query(t):
    return (lax.broadcasted_iota(jnp.int32, (t, t), 0) - lax.broadcasted_iota(jnp.int32, (t, t), 1))


def _chunk_plan(qi):
    n_full = qi // ATTN_CHUNK
    lo = n_full * ATTN_CHUNK
    win = jnp.maximum(qi - (ATTN_CHUNK - 1), 0)
    return n_full, lo, win


def _diff_kernel(lam_ref, g_ref, qt_ref, k_ref, vt_ref, o_ref, acc1_ref, acc2_ref,
                 sx_ref, sy_ref, mxx_ref, mxy_ref, *, lam_init):
    qi = pl.program_id(2)
    qt = qt_ref[0, 0].astype(F32)
    t = qt.shape[1]
    row = lax.broadcasted_iota(jnp.int32, qt.shape, 0)
    q1 = jnp.where(row < HEAD_DIM, qt, 0.0).astype(BF16)
    q2 = jnp.where(row >= HEAD_DIM, qt, 0.0).astype(BF16)

    n_full, lo, win = _chunk_plan(qi)
    kmq = _key_minus_query(t)

    def window_mask(w, s):
        j = win + w
        limit = jnp.where(j >= lo, (qi - j) * t, -2 * t)
        return jnp.where(kmq <= limit, s, NEG)

    no_mask = lambda w, s: s
    rows = slice(None)
    _, l1, _, l2 = _flash_pipeline((n_full, win), k_ref, vt_ref, (q1, q2), (rows, rows),
                                   (window_mask, window_mask), lambda first: (no_mask, no_mask),
                                   (sx_ref, sy_ref), (mxx_ref, mxy_ref), (acc1_ref, acc2_ref))

    lp = lam_ref[...]
    lam = (jnp.exp(jnp.sum(lp[0:1] * lp[1:2], axis=1, keepdims=True))
           - jnp.exp(jnp.sum(lp[2:3] * lp[3:4], axis=1, keepdims=True)) + lam_init)
    o = acc1_ref[...] / l1 - lam * (acc2_ref[...] / l2)
    o = o * lax.rsqrt(jnp.mean(o * o, axis=0, keepdims=True) + EPS) * g_ref[...]
    o_ref[0] = (o * (1.0 - lam_init)).T.astype(o_ref.dtype)


def _diff_attention(qt, k, vt, lam_params, subln_g, lam_init):
    bsz, ns, wq, t = qt.shape
    s = k.shape[1]
    dv = 2 * HEAD_DIM
    nh = wq // dv
    kern = functools.partial(_diff_kernel, lam_init=lam_init)
    in_specs = [
        _const_spec(lam_params.shape),
        _const_spec((dv, 1)),
        pl.BlockSpec((1, 1, dv, t), lambda b, h, i: (b, i, h, 0)),
        pl.BlockSpec((1, s, dv), lambda b, h, i: (b, 0, h)),
        pl.BlockSpec((1, ns, dv, t), lambda b, h, i: (b, 0, h, 0)),
    ]
    return pl.pallas_call(
        kern, grid=(bsz, nh, ns), in_specs=in_specs,
        out_specs=pl.BlockSpec((1, t, dv), lambda b, h, i: (b, i, h)),
        out_shape=jax.ShapeDtypeStruct((bsz, s, wq), BF16),
        scratch_shapes=[pltpu.VMEM((dv, t), F32), pltpu.VMEM((dv, t), F32)] + _score_scratch(t),
        compiler_params=_params("parallel", "parallel", "arbitrary"), name="diff_attention",
    )(lam_params, subln_g.reshape(dv, 1), qt, k, vt)


def _moba_kernel(g_ref, qt_ref, k_ref, vt_ref, o_ref, kmean_ref, bias_ref, acc_ref,
                 sx_ref, sy_ref, mxx_ref, mxy_ref, *, n_blocks):
    qi = pl.program_id(2)
    blk = MOBA_BLOCK
    pair = 2 * HEAD_DIM

    @pl.when(qi == 0)
    def _():
        def mean_block(j, c):
            kb = k_ref[0, pl.ds(pl.multiple_of(j * blk, blk), blk), :].astype(F32)
            kmean_ref[pl.ds(j, 1), :] = jnp.mean(kb, axis=0, keepdims=True)
            return c
        lax.fori_loop(0, n_blocks, mean_block, 0)

    qt = qt_ref[0, 0].astype(F32)
    row = lax.broadcasted_iota(jnp.int32, qt.shape, 0)
    lane = lax.broadcasted_iota(jnp.int32, (n_blocks, pair), 1)
    bid = lax.broadcasted_iota(jnp.int32, (n_blocks, blk), 0).astype(F32)
    past = bid < qi.astype(F32)
    kmean = kmean_ref[...]
    qs = []
    for hh in range(2):
        in_head = (row >= hh * HEAD_DIM) & (row < (hh + 1) * HEAD_DIM)
        qf = jnp.where(in_head, qt, 0.0)
        qs.append(qf.astype(BF16))
        km = jnp.where((lane >= hh * HEAD_DIM) & (lane < (hh + 1) * HEAD_DIM), kmean, 0.0)
        gate = jnp.dot(km, qf, preferred_element_type=F32, precision=lax.Precision.HIGHEST)
        gate = jnp.where(past, gate, NEG)
        sel = jnp.zeros_like(gate)
        for _ in range(MOBA_TOPK):
            top = jnp.max(gate, axis=0, keepdims=True)
            first = jnp.min(jnp.where(gate == top, bid, float(n_blocks)), axis=0, keepdims=True)
            hit = bid == first
            sel = jnp.where(hit, 1.0, sel)
            gate = jnp.where(hit, BELOW_NEG, gate)
        bias_ref[hh] = jnp.where(past, jnp.where(sel > 0.0, 0.0, NEG), NEG)

    n_full, lo, win = _chunk_plan(qi)
    kmq = _key_minus_query(blk)
    v_rows = [slice(hh * HEAD_DIM, (hh + 1) * HEAD_DIM) for hh in range(2)]

    def window_mask(hh):
        def mask(w, s):
            j = win + w
            picked = jnp.where(j >= lo, bias_ref[hh, pl.ds(j, 1), :], NEG)
            own = jnp.where(j == qi, 0, -2 * blk)
            return jnp.where(kmq <= own, s, s + picked)
        return mask

    def chunk_mask(hh, first):
        return lambda w, s: s + bias_ref[hh, pl.ds(first + w, 1), :]

    ml = _flash_pipeline((n_full, win), k_ref, vt_ref, qs, v_rows,
                         (window_mask(0), window_mask(1)),
                         lambda first: (chunk_mask(0, first), chunk_mask(1, first)),
                         (sx_ref, sy_ref), (mxx_ref, mxy_ref), (acc_ref.at[0], acc_ref.at[1]))
    outs = []
    for hh in range(2):
        o = acc_ref[hh] / ml[2 * hh + 1]
        outs.append(o * lax.rsqrt(jnp.mean(o * o, axis=0, keepdims=True) + EPS) * g_ref[...])
    o_ref[0] = jnp.concatenate(outs, axis=0).T.astype(o_ref.dtype)


def _moba_attention(qt, k, vt, norm_g):
    bsz, nb, wq, blk = qt.shape
    s = k.shape[1]
    pair = 2 * HEAD_DIM
    kern = functools.partial(_moba_kernel, n_blocks=nb)
    in_specs = [
        _const_spec((HEAD_DIM, 1)),
        pl.BlockSpec((1, 1, pair, blk), lambda b, h, i: (b, i, h, 0)),
        pl.BlockSpec((1, s, pair), lambda b, h, i: (b, 0, h)),
        pl.BlockSpec((1, nb, pair, blk), lambda b, h, i: (b, 0, h, 0)),
    ]
    return pl.pallas_call(
        kern, grid=(bsz, wq // pair, nb), in_specs=in_specs,
        out_specs=pl.BlockSpec((1, blk, pair), lambda b, h, i: (b, i, h)),
        out_shape=jax.ShapeDtypeStruct((bsz, s, wq), BF16),
        scratch_shapes=[pltpu.VMEM((nb, pair), F32), pltpu.VMEM((2, nb, blk), F32),
                        pltpu.VMEM((2, HEAD_DIM, blk), F32)] + _score_scratch(blk),
        compiler_params=_params("parallel", "parallel", "arbitrary"), name="moba_attention",
    )(norm_g.reshape(HEAD_DIM, 1), qt, k, vt)


def _out_mlp_kernel(x_ref, ys_ref, yd_ref, ym_ref, mod_ref, g2_ref, wo_ref, w1_ref, w2_ref, fg_ref,
                    o_ref, *, final_norm, ff_chunk):
    x = x_ref[0]
    mod = mod_ref[0]
    ws = ys_ref.shape[2]
    wd = yd_ref.shape[2]
    mix = (jnp.dot(ys_ref[0], wo_ref[0:ws], preferred_element_type=F32)
           + jnp.dot(yd_ref[0], wo_ref[ws:ws + wd], preferred_element_type=F32)
           + jnp.dot(ym_ref[0], wo_ref[ws + wd:], preferred_element_type=F32))
    x = x + mod[2:3] * mix
    h = (_rmsnorm_rows(x, g2_ref[...]) * (1.0 + mod[4:5]) + mod[3:4]).astype(BF16)
    acc = jnp.zeros_like(x)
    for c in range(w1_ref.shape[1] // ff_chunk):
        a = jnp.dot(h, w1_ref[:, c * ff_chunk:(c + 1) * ff_chunk], preferred_element_type=F32)
        a = jnp.square(jnp.maximum(a, 0.0)).astype(BF16)
        acc = acc + jnp.dot(a, w2_ref[c * ff_chunk:(c + 1) * ff_chunk, :], preferred_element_type=F32)
    x = x + mod[5:6] * acc
    if final_norm:
        x = _rmsnorm_rows(x, fg_ref[...])
    o_ref[0] = x


def _out_mlp(x, ys, yd, ym, mod, g2, wo, w1, w2, final_g, final_norm):
    bsz, s, d = x.shape
    tm = TOKEN_TILE
    kern = functools.partial(_out_mlp_kernel, final_norm=final_norm, ff_chunk=d)
    tok = lambda w: pl.BlockSpec((1, tm, w), lambda b, i: (b, i, 0))
    in_specs = [
        tok(d), tok(ys.shape[2]), tok(yd.shape[2]), tok(ym.shape[2]),
        pl.BlockSpec((1, 6, d), lambda b, i: (b, 0, 0)),
        _const_spec((1, d)), _const_spec(wo.shape), _const_spec(w1.shape), _const_spec(w2.shape),
        _const_spec((1, d)),
    ]
    return pl.pallas_call(
        kern, grid=(bsz, s // tm), in_specs=in_specs, out_specs=tok(d),
        out_shape=jax.ShapeDtypeStruct((bsz, s, d), F32),
        compiler_params=_params("parallel", "parallel"), name="out_mlp",
    )(x, ys, yd, ym, mod, g2, wo, w1, w2, final_g)


def kernel(x, c, positions, norm1_g, norm2_g, w_ada, b_ada, w_in, w_out, ssm_a_re, ssm_a_im, ssm_log_dt, ssm_b_re, ssm_b_im, ssm_c_re, ssm_c_im, ssm_d, ssm_glu_w, ssm_glu_b, ssm_norm_g, diff_lq1, diff_lk1, diff_lq2, diff_lk2, diff_subln_g, moba_norm_g, mlp_w1, mlp_w2, final_g):
    bsz, s, d = x.shape
    depth = w_in.shape[0]
    w_ssm = ssm_d.shape[1]
    w_diff = DIFF_HEADS * 2 * HEAD_DIM
    w_moba = MOBA_HEADS * HEAD_DIM
    assert s % TOKEN_TILE == 0 and TOKEN_TILE % ATTN_BLOCK == 0 and s % (ATTN_CHUNK * ATTN_BLOCK) == 0
    assert w_in.shape[2] == w_ssm + 3 * w_diff + 3 * w_moba

    cos_t, sin_t = _rope_tables(positions)
    mod_all = _ada_mod(c, w_ada, b_ada).reshape(depth, bsz, 6, d)

    o_dq = w_ssm
    o_dk = o_dq + w_diff
    o_dv = o_dk + w_diff
    o_mq = o_dv + w_diff
    o_mk = o_mq + w_moba
    o_mv = o_mk + w_moba
    widths = (w_diff, w_diff, w_moba, w_moba, w_diff, w_moba)

    for l in range(depth):
        wl = w_in[l]
        wu = wl[:, :o_dq].astype(BF16)
        wt = jnp.concatenate([wl[:, o_dq:o_dk], wl[:, o_dk:o_dv], wl[:, o_mq:o_mk], wl[:, o_mk:o_mv],
                              wl[:, o_dv:o_mq], wl[:, o_mv:]], axis=1).T.astype(BF16)
        mod = mod_all[l]
        u, dq, dk, dv, mq, mk, mv = _projection(x, mod, norm1_g[l][None], wu, wt, cos_t, sin_t, widths)

        b_blk, c_blk, coef = _ssm_prep(ssm_a_re[l], ssm_a_im[l], ssm_log_dt[l], ssm_b_re[l], ssm_b_im[l],
                                       ssm_c_re[l], ssm_c_im[l])
        y_ssm = _ssm(u, b_blk, c_blk, coef, ssm_d[l][None], ssm_glu_w[l].astype(BF16),
                     ssm_glu_b[l][None], ssm_norm_g[l][None])

        lam_init = 0.8 - 0.6 * math.exp(-0.3 * l)
        lam_params = jnp.stack([diff_lq1[l], diff_lk1[l], diff_lq2[l], diff_lk2[l]]).astype(F32)
        y_diff = _diff_attention(dq, dk, dv, lam_params, diff_subln_g[l], lam_init)
        y_moba = _moba_attention(mq, mk, mv, moba_norm_g[l])

        x = _out_mlp(x, y_ssm, y_diff, y_moba, mod, norm2_g[l][None], w_out[l].astype(BF16),
                     mlp_w1[l].astype(BF16), mlp_w2[l].astype(BF16), final_g[None],
                     final_norm=(l == depth - 1))
    return x
```

```python
import functools
import math

import jax
import jax.numpy as jnp
from jax import lax
from jax.experimental import pallas as pl
from jax.experimental.pallas import tpu as pltpu

F32 = jnp.float32
BF16 = jnp.bfloat16

SSM_GROUP = 16
SSM_STATE = 64
HEAD_DIM = 64
DIFF_HEADS = 4
MOBA_HEADS = 4
MOBA_BLOCK = 256
MOBA_TOPK = 3
ROPE_THETA = 500000.0
ROPE_HALF = 8
EPS = 1e-6
NEG = -1e30
BELOW_NEG = -3e38

SUBLANES = 8
TOKEN_TILE = 512
ATTN_BLOCK = MOBA_BLOCK
ATTN_CHUNK = 4
VMEM_LIMIT = 56 * 1024 * 1024


def _params(*sem):
    return pltpu.CompilerParams(dimension_semantics=sem, vmem_limit_bytes=VMEM_LIMIT)


def _const_spec(shape):
    nd = len(shape)
    return pl.BlockSpec(shape, lambda *_: (0,) * nd)


def _rope_kernel(inv_ref, pos_ref, cos_ref, sin_ref):
    ang = inv_ref[...] * pos_ref[0].astype(F32)
    cos_ref[0] = jnp.cos(ang)
    sin_ref[0] = jnp.sin(ang)


def _rope_tables(positions):
    bsz, s = positions.shape
    rot = 2 * ROPE_HALF
    inv = (ROPE_THETA ** (-jnp.arange(0, rot, 2, dtype=F32) / rot)).reshape(ROPE_HALF, 1)
    out = jax.ShapeDtypeStruct((bsz, ROPE_HALF, s), F32)
    return pl.pallas_call(
        _rope_kernel,
        grid=(bsz,),
        in_specs=[_const_spec((ROPE_HALF, 1)),
                  pl.BlockSpec((1, 1, s), lambda b: (b, 0, 0))],
        out_specs=[pl.BlockSpec((1, ROPE_HALF, s), lambda b: (b, 0, 0))] * 2,
        out_shape=[out, out],
        compiler_params=_params("parallel"),
        name="rope_tables",
    )(inv, positions.reshape(bsz, 1, s))


def _ada_kernel(c_ref, w_ref, b_ref, o_ref):
    c = c_ref[...]
    sc = c * jax.nn.sigmoid(c)
    o_ref[0] = jnp.dot(sc, w_ref[0], preferred_element_type=F32,
                       precision=lax.Precision.HIGHEST) + b_ref[0]


def _ada_mod(c, w_ada, b_ada):
    depth, d, d6 = w_ada.shape
    bsz = c.shape[0]
    nchunk = d6 // d
    return pl.pallas_call(
        _ada_kernel,
        grid=(depth, nchunk),
        in_specs=[_const_spec((bsz, d)),
                  pl.BlockSpec((1, d, d), lambda l, j: (l, 0, j)),
                  pl.BlockSpec((1, 1, d), lambda l, j: (l, 0, j))],
        out_specs=pl.BlockSpec((1, bsz, d), lambda l, j: (l, 0, j)),
        out_shape=jax.ShapeDtypeStruct((depth, bsz, d6), F32),
        compiler_params=_params("parallel", "parallel"),
        name="ada_mod",
    )(c, w_ada, b_ada.reshape(depth, 1, d6))


def _ssm_prep_kernel(are_ref, aim_ref, ldt_ref, bre_ref, bim_ref,
                     bbr_ref, bbi_ref, pwr_ref, pwi_ref):
    ar = are_ref[...]
    ai = aim_ref[...]
    dt = jnp.exp(ldt_ref[...])
    mag = jnp.exp(ar * dt)
    abr = mag * jnp.cos(ai * dt)
    abi = mag * jnp.sin(ai * dt)
    den = ar * ar + ai * ai
    nr = abr - 1.0
    ni = abi
    kr = (nr * ar + ni * ai) / den
    ki = (ni * ar - nr * ai) / den
    br = bre_ref[...]
    bi = bim_ref[...]
    bbr_ref[...] = kr[None] * br - ki[None] * bi
    bbi_ref[...] = kr[None] * bi + ki[None] * br
    pr, pi = abr, abi
    pwr_ref[0] = pr
    pwi_ref[0] = pi
    for j in range(1, SUBLANES):
        pr, pi = pr * abr - pi * abi, pr * abi + pi * abr
        pwr_ref[j] = pr
        pwi_ref[j] = pi


def _ssm_prep(a_re, a_im, log_dt, b_re, b_im, c_re, c_im):
    g, p = a_re.shape
    cch = b_re.shape[-1]
    w = g * cch
    n = g * p
    bt = lambda b: jnp.transpose(b, (2, 0, 1))
    shapes = [jax.ShapeDtypeStruct((cch, g, p), F32)] * 2 + [jax.ShapeDtypeStruct((SUBLANES, g, p), F32)] * 2
    bbr, bbi, pwr, pwi = pl.pallas_call(
        _ssm_prep_kernel, out_shape=shapes, name="ssm_prep",
    )(a_re, a_im, log_dt.reshape(g, 1), bt(b_re), bt(b_im))
    eye = jnp.eye(g, dtype=F32)
    blk_b = lambda bb: jnp.einsum('cgp,gh->gchp', bb, eye).reshape(w, n)
    b_blk = jnp.concatenate([blk_b(bbr), blk_b(bbi)], axis=1).astype(BF16)
    blk_c = lambda cc: jnp.einsum('gcp,gh->gphc', cc, eye).reshape(n, w)
    c_blk = jnp.concatenate([blk_c(c_re), -blk_c(c_im)], axis=0).astype(BF16)
    pwr = pwr.reshape(SUBLANES, n)
    pwi = pwi.reshape(SUBLANES, n)
    row = jnp.arange(SUBLANES)[:, None]
    tabs = []
    for shift in (1, 2, 4):
        keep = row >= shift
        tabs.append(jnp.where(keep, pwr[shift - 1][None], 0.0))
        tabs.append(jnp.where(keep, pwi[shift - 1][None], 0.0))
    tabs += [pwr, pwi]
    return b_blk, c_blk, jnp.stack(tabs)


def _rmsnorm_rows(x, g):
    return x * lax.rsqrt(jnp.mean(x * x, axis=-1, keepdims=True) + EPS) * g


def _proj_kernel(x_ref, mod_ref, g_ref, wu_ref, wt_ref, cos_ref, sin_ref,
                 u_ref, dq_ref, dk_ref, dv_ref, mq_ref, mk_ref, mv_ref, *, n_rot, widths):
    wdq, wdk, wmq, wmk, wdv, wmv = widths
    x = x_ref[0]
    mod = mod_ref[0]
    h = _rmsnorm_rows(x, g_ref[...]) * (1.0 + mod[1:2]) + mod[0:1]
    hb = h.astype(BF16)
    u_ref[0] = jnp.dot(hb, wu_ref[...], preferred_element_type=F32)
    pt = lax.dot_general(wt_ref[...], hb, (((1,), (1,)), ((), ())), preferred_element_type=F32)
    tm = pt.shape[1]
    rot = pt[:n_rot].reshape(n_rot // HEAD_DIM, HEAD_DIM, tm)
    cos = cos_ref[0][None]
    sin = sin_ref[0][None]
    x1 = rot[:, 0:ROPE_HALF]
    x2 = rot[:, ROPE_HALF:2 * ROPE_HALF]
    rot = jnp.concatenate([x1 * cos - x2 * sin, x2 * cos + x1 * sin, rot[:, 2 * ROPE_HALF:]], axis=1)
    rot = rot.reshape(n_rot, tm)
    qscale = HEAD_DIM ** -0.5
    o = 0
    dq = (rot[o:o + wdq] * qscale).astype(BF16); o += wdq
    dk_ref[0] = rot[o:o + wdk].T.astype(BF16); o += wdk
    mq = (rot[o:o + wmq] * qscale).astype(BF16); o += wmq
    mk_ref[0] = rot[o:o + wmk].T.astype(BF16); o += wmk
    dv = pt[o:o + wdv].astype(BF16); o += wdv
    mv = pt[o:o + wmv].astype(BF16)
    for j in range(tm // ATTN_BLOCK):
        blk = slice(j * ATTN_BLOCK, (j + 1) * ATTN_BLOCK)
        dq_ref[0, j] = dq[:, blk]
        dv_ref[0, j] = dv[:, blk]
        mq_ref[0, j] = mq[:, blk]
        mv_ref[0, j] = mv[:, blk]


def _projection(x, mod, g, wu, wt, cos_t, sin_t, widths):
    bsz, s, d = x.shape
    tm = TOKEN_TILE
    ns = s // tm
    wdq, wdk, wmq, wmk, wdv, wmv = widths
    n_rot = wdq + wdk + wmq + wmk
    ab = ATTN_BLOCK
    nab = tm // ab
    kern = functools.partial(_proj_kernel, n_rot=n_rot, widths=widths)
    tiles_t = lambda w: jax.ShapeDtypeStruct((bsz, s // ab, w, ab), BF16)
    out_shape = [
        jax.ShapeDtypeStruct((bsz, s, wu.shape[1]), F32),
        tiles_t(wdq),
        jax.ShapeDtypeStruct((bsz, s, wdk), BF16),
        tiles_t(wdv),
        tiles_t(wmq),
        jax.ShapeDtypeStruct((bsz, s, wmk), BF16),
        tiles_t(wmv),
    ]
    spec_t = lambda w: pl.BlockSpec((1, nab, w, ab), lambda b, i: (b, i, 0, 0))
    out_specs = [
        pl.BlockSpec((1, tm, wu.shape[1]), lambda b, i: (b, i, 0)),
        spec_t(wdq),
        pl.BlockSpec((1, tm, wdk), lambda b, i: (b, i, 0)),
        spec_t(wdv),
        spec_t(wmq),
        pl.BlockSpec((1, tm, wmk), lambda b, i: (b, i, 0)),
        spec_t(wmv),
    ]
    in_specs = [
        pl.BlockSpec((1, tm, d), lambda b, i: (b, i, 0)),
        pl.BlockSpec((1, 6, d), lambda b, i: (b, 0, 0)),
        _const_spec((1, d)),
        _const_spec(wu.shape),
        _const_spec(wt.shape),
        pl.BlockSpec((1, ROPE_HALF, tm), lambda b, i: (b, 0, i)),
        pl.BlockSpec((1, ROPE_HALF, tm), lambda b, i: (b, 0, i)),
    ]
    return pl.pallas_call(
        kern, grid=(bsz, ns), in_specs=in_specs, out_specs=out_specs, out_shape=out_shape,
        compiler_params=_params("parallel", "parallel"), name="projection",
    )(x, mod, g, wu, wt, cos_t, sin_t)


def _gelu_tanh(y):
    return 0.5 * y * (1.0 + jnp.tanh(math.sqrt(2.0 / math.pi) * (y + 0.044715 * (y * y * y))))


def _ssm_kernel(u_ref, bblk_ref, cblk_ref, coef_ref, d_ref, gw_ref, gb_ref, ng_ref,
                y_ref, bu_ref, carry_ref, *, n_state):
    n = n_state

    @pl.when(pl.program_id(1) == 0)
    def _():
        carry_ref[...] = jnp.zeros_like(carry_ref)

    u = u_ref[0]
    tm = u.shape[0]
    bu_ref[...] = jnp.dot(u.astype(BF16), bblk_ref[...], preferred_element_type=F32)

    def cmul_add(xr, xi, ar, ai, sr, si):
        return xr + (ar * sr - ai * si), xi + (ar * si + ai * sr)

    def group(r, carry):
        cr, ci = carry
        rows = pl.ds(pl.multiple_of(r * SUBLANES, SUBLANES), SUBLANES)
        xr = bu_ref[rows, 0:n]
        xi = bu_ref[rows, n:2 * n]
        for lvl, shift in enumerate((1, 2, 4)):
            sr = pltpu.roll(xr, shift, axis=0)
            si = pltpu.roll(xi, shift, axis=0)
            xr, xi = cmul_add(xr, xi, coef_ref[2 * lvl], coef_ref[2 * lvl + 1], sr, si)
        xr, xi = cmul_add(xr, xi, coef_ref[6], coef_ref[7], cr, ci)
        bu_ref[rows, 0:n] = xr
        bu_ref[rows, n:2 * n] = xi
        last = SUBLANES - 1
        return (jnp.broadcast_to(xr[last:last + 1], xr.shape),
                jnp.broadcast_to(xi[last:last + 1], xi.shape))

    cr, ci = lax.fori_loop(0, tm // SUBLANES, group, (carry_ref[0], carry_ref[1]))
    carry_ref[0] = cr
    carry_ref[1] = ci

    y = jnp.dot(bu_ref[...].astype(BF16), cblk_ref[...], preferred_element_type=F32)
    y = y + d_ref[...] * u
    y = _gelu_tanh(y)
    z = jnp.dot(y.astype(BF16), gw_ref[...], preferred_element_type=F32) + gb_ref[...]
    y = y * jax.nn.sigmoid(z)
    y_ref[0] = _rmsnorm_rows(y, ng_ref[...]).astype(y_ref.dtype)


def _ssm(u, b_blk, c_blk, coef, d_skip, glu_w, glu_b, norm_g):
    bsz, s, w = u.shape
    n = coef.shape[-1]
    tm = TOKEN_TILE
    kern = functools.partial(_ssm_kernel, n_state=n)
    in_specs = [
        pl.BlockSpec((1, tm, w), lambda b, i: (b, i, 0)),
        _const_spec(b_blk.shape), _const_spec(c_blk.shape), _const_spec(coef.shape),
        _const_spec((1, w)), _const_spec((w, w)), _const_spec((1, w)), _const_spec((1, w)),
    ]
    return pl.pallas_call(
        kern, grid=(bsz, s // tm), in_specs=in_specs,
        out_specs=pl.BlockSpec((1, tm, w), lambda b, i: (b, i, 0)),
        out_shape=jax.ShapeDtypeStruct((bsz, s, w), BF16),
        scratch_shapes=[pltpu.VMEM((tm, 2 * n), F32), pltpu.VMEM((2, SUBLANES, n), F32)],
        compiler_params=_params("parallel", "arbitrary"), name="ssm",
    )(u, b_blk, c_blk, coef, d_skip, glu_w, glu_b, norm_g)


def _row_groups(x):
    rows, lanes = x.shape
    return x.reshape(4, rows // (4 * SUBLANES), SUBLANES, lanes)


def _score_stage(k_ref, first_blk, n_gran, qz, mask_fn, s_ref, mx_ref):
    ab = ATTN_BLOCK
    for w in range(n_gran):
        kb = k_ref[0, pl.ds(pl.multiple_of((first_blk + w) * ab, ab), ab), :]
        s = mask_fn(w, jnp.dot(kb, qz, preferred_element_type=F32))
        s_ref[w * ab:(w + 1) * ab, :] = s
        mx_ref[w] = jnp.max(jnp.max(_row_groups(s), axis=1), axis=0)


def _softmax_stage(s_ref, mx_ref, vt_ref, first_blk, n_gran, v_rows, m, l, acc_ref):
    ab = ATTN_BLOCK
    m_new = jnp.maximum(m, jnp.max(jnp.max(mx_ref[0:n_gran], axis=0), axis=0, keepdims=True))
    alpha = jnp.exp(m - m_new)
    psum = None
    pv = None
    for w in range(n_gran):
        p = jnp.exp(s_ref[w * ab:(w + 1) * ab, :] - m_new)
        part = jnp.sum(jnp.sum(_row_groups(p), axis=1), axis=0)
        psum = part if psum is None else psum + part
        d = jnp.dot(vt_ref[0, first_blk + w, v_rows, :], p.astype(BF16), preferred_element_type=F32)
        pv = d if pv is None else pv + d
    acc_ref[...] = alpha * acc_ref[...] + pv
    return m_new, alpha * l + jnp.sum(psum, axis=0, keepdims=True)


def _flash_tile(g, k_ref, vt_ref, chains, full_mask, diag_mask, qz_ref, s_ref, mx_ref, acc_ref):
    nc = len(chains)
    t = s_ref.shape[-1]
    acc_ref[...] = jnp.zeros(acc_ref.shape, F32)
    neg = jnp.full((1, t), NEG, F32)
    zero = jnp.zeros((1, t), F32)

    def score(c, first, n_gran, mask_fn):
        _score_stage(k_ref, first, n_gran, qz_ref[c], mask_fn, s_ref.at[c], mx_ref.at[c])

    def softmax(c, first, n_gran, m, l):
        return _softmax_stage(s_ref.at[c], mx_ref.at[c], vt_ref, first, n_gran, chains[c][1], m, l, acc_ref.at[c])

    score(0, 0, ATTN_CHUNK, full_mask(0, 0))

    def step(i, carry):
        first = i * ATTN_CHUNK
        out = []
        for c in range(nc):
            if c + 1 < nc:
                score(c + 1, first, ATTN_CHUNK, full_mask(c + 1, first))
            else:
                score(0, first + ATTN_CHUNK, ATTN_CHUNK, full_mask(0, first + ATTN_CHUNK))
            out += softmax(c, first, ATTN_CHUNK, carry[2 * c], carry[2 * c + 1])
        return tuple(out)

    carry = lax.fori_loop(0, g, step, (neg, zero) * nc)
    first = g * ATTN_CHUNK
    out = []
    score(0, first, chains[0][0] + 1, diag_mask(0))
    for c in range(nc):
        if c + 1 < nc:
            score(c + 1, first, chains[c + 1][0] + 1, diag_mask(c + 1))
        out.append(softmax(c, first, chains[c][0] + 1, carry[2 * c], carry[2 * c + 1]))
    return out


def _flash_scratch(nc, dv):
    ab = ATTN_BLOCK
    return [pltpu.VMEM((nc, 2 * HEAD_DIM, ab), BF16),
            pltpu.VMEM((nc, ATTN_CHUNK * ab, ab), F32),
            pltpu.VMEM((nc, ATTN_CHUNK, SUBLANES, ab), F32),
            pltpu.VMEM((nc, dv, ab), F32),
            pltpu.VMEM((ab, ab), F32)]


def _causal_bias(t):
    key = lax.broadcasted_iota(jnp.int32, (t, t), 0)
    qry = lax.broadcasted_iota(jnp.int32, (t, t), 1)
    return jnp.where(key <= qry, 0.0, NEG)


def _diff_kernel(lam_ref, g_ref, qt_ref, k_ref, vt_ref, o_ref, qz_ref, s_ref, mx_ref, acc_ref, tri_ref,
                 *, lam_init):
    g = pl.program_id(2)
    ab = ATTN_BLOCK
    chains = [(qb, slice(None)) for qb in range(ATTN_CHUNK) for _ in range(2)]
    tri_ref[...] = _causal_bias(ab)
    row = lax.broadcasted_iota(jnp.int32, (2 * HEAD_DIM, ab), 0)
    for qb in range(ATTN_CHUNK):
        qt = qt_ref[0, qb].astype(F32)
        qz_ref[2 * qb] = jnp.where(row < HEAD_DIM, qt, 0.0).astype(BF16)
        qz_ref[2 * qb + 1] = jnp.where(row >= HEAD_DIM, qt, 0.0).astype(BF16)

    no_mask = lambda w, s: s

    def diag_mask(c):
        qb = chains[c][0]
        return lambda w, s: s + tri_ref[...] if w == qb else s

    ml = _flash_tile(g, k_ref, vt_ref, chains, lambda c, first: no_mask, diag_mask,
                     qz_ref, s_ref, mx_ref, acc_ref)

    lp = lam_ref[...]
    lam = (jnp.exp(jnp.sum(lp[0:1] * lp[1:2], axis=1, keepdims=True))
           - jnp.exp(jnp.sum(lp[2:3] * lp[3:4], axis=1, keepdims=True)) + lam_init)
    for qb in range(ATTN_CHUNK):
        o = acc_ref[2 * qb] / ml[2 * qb][1] - lam * (acc_ref[2 * qb + 1] / ml[2 * qb + 1][1])
        o = o * lax.rsqrt(jnp.mean(o * o, axis=0, keepdims=True) + EPS) * g_ref[...]
        o_ref[0, qb * ab:(qb + 1) * ab, :] = (o * (1.0 - lam_init)).T.astype(o_ref.dtype)


def _diff_attention(qt, k, vt, lam_params, subln_g, lam_init):
    bsz, nb, wq, ab = qt.shape
    s = k.shape[1]
    dv = 2 * HEAD_DIM
    nh = wq // dv
    tq = ATTN_CHUNK * ab
    kern = functools.partial(_diff_kernel, lam_init=lam_init)
    in_specs = [
        _const_spec(lam_params.shape),
        _const_spec((dv, 1)),
        pl.BlockSpec((1, ATTN_CHUNK, dv, ab), lambda b, h, i: (b, i, h, 0)),
        pl.BlockSpec((1, s, dv), lambda b, h, i: (b, 0, h)),
        pl.BlockSpec((1, nb, dv, ab), lambda b, h, i: (b, 0, h, 0)),
    ]
    return pl.pallas_call(
        kern, grid=(bsz, nh, nb // ATTN_CHUNK), in_specs=in_specs,
        out_specs=pl.BlockSpec((1, tq, dv), lambda b, h, i: (b, i, h)),
        out_shape=jax.ShapeDtypeStruct((bsz, s, wq), BF16),
        scratch_shapes=_flash_scratch(2 * ATTN_CHUNK, dv),
        compiler_params=_params("parallel", "parallel", "arbitrary"), name="diff_attention",
    )(lam_params, subln_g.reshape(dv, 1), qt, k, vt)


def _moba_kernel(g_ref, qt_ref, k_ref, vt_ref, o_ref, kmean_ref, bias_ref, qz_ref, s_ref, mx_ref, acc_ref,
                 tri_ref, *, n_blocks):
    g = pl.program_id(2)
    ab = ATTN_BLOCK
    tq = ATTN_CHUNK * ab
    pair = 2 * HEAD_DIM

    @pl.when(g == 0)
    def _():
        def mean_block(j, c):
            kb = k_ref[0, pl.ds(pl.multiple_of(j * ab, ab), ab), :].astype(F32)
            kmean_ref[pl.ds(j, 1), :] = jnp.mean(kb, axis=0, keepdims=True)
            return c
        lax.fori_loop(0, n_blocks, mean_block, 0)

    tri_ref[...] = _causal_bias(ab)
    chains = [(qb, slice(hh * HEAD_DIM, (hh + 1) * HEAD_DIM)) for qb in range(ATTN_CHUNK) for hh in range(2)]
    qt = jnp.concatenate([qt_ref[0, qb] for qb in range(ATTN_CHUNK)], axis=1).astype(F32)
    row = lax.broadcasted_iota(jnp.int32, (pair, tq), 0)
    lane = lax.broadcasted_iota(jnp.int32, (n_blocks, pair), 1)
    bid = lax.broadcasted_iota(jnp.int32, (n_blocks, tq), 0)
    qblk = g * ATTN_CHUNK + lax.broadcasted_iota(jnp.int32, (n_blocks, tq), 1) // ab
    past = bid < qblk
    bid = bid.astype(F32)
    kmean = kmean_ref[...]
    for hh in range(2):
        qf = jnp.where((row >= hh * HEAD_DIM) & (row < (hh + 1) * HEAD_DIM), qt, 0.0)
        for qb in range(ATTN_CHUNK):
            qz_ref[2 * qb + hh] = qf[:, qb * ab:(qb + 1) * ab].astype(BF16)
        km = jnp.where((lane >= hh * HEAD_DIM) & (lane < (hh + 1) * HEAD_DIM), kmean, 0.0)
        gate = jnp.dot(km, qf, preferred_element_type=F32, precision=lax.Precision.HIGHEST)
        gate = jnp.where(past, gate, NEG)
        sel = jnp.zeros_like(gate)
        for _ in range(MOBA_TOPK):
            top = jnp.max(gate, axis=0, keepdims=True)
            first = jnp.min(jnp.where(gate == top, bid, float(n_blocks)), axis=0, keepdims=True)
            hit = bid == first
            sel = jnp.where(hit, 1.0, sel)
            gate = jnp.where(hit, BELOW_NEG, gate)
        bias_ref[hh] = jnp.where(past, jnp.where(sel > 0.0, 0.0, NEG), NEG)

    def picked(c, blk, s):
        qb, hh = c // 2, c % 2
        return s + bias_ref[hh, pl.ds(blk, 1), qb * ab:(qb + 1) * ab]

    def full_mask(c, first):
        return lambda w, s: picked(c, first + w, s)

    def diag_mask(c):
        qb = chains[c][0]
        return lambda w, s: s + tri_ref[...] if w == qb else picked(c, g * ATTN_CHUNK + w, s)

    ml = _flash_tile(g, k_ref, vt_ref, chains, full_mask, diag_mask, qz_ref, s_ref, mx_ref, acc_ref)

    for qb in range(ATTN_CHUNK):
        outs = []
        for hh in range(2):
            o = acc_ref[2 * qb + hh] / ml[2 * qb + hh][1]
            outs.append(o * lax.rsqrt(jnp.mean(o * o, axis=0, keepdims=True) + EPS) * g_ref[...])
        o_ref[0, qb * ab:(qb + 1) * ab, :] = jnp.concatenate(outs, axis=0).T.astype(o_ref.dtype)


def _moba_attention(qt, k, vt, norm_g):
    bsz, nb, wq, ab = qt.shape
    s = k.shape[1]
    pair = 2 * HEAD_DIM
    tq = ATTN_CHUNK * ab
    kern = functools.partial(_moba_kernel, n_blocks=nb)
    in_specs = [
        _const_spec((HEAD_DIM, 1)),
        pl.BlockSpec((1, ATTN_CHUNK, pair, ab), lambda b, h, i: (b, i, h, 0)),
        pl.BlockSpec((1, s, pair), lambda b, h, i: (b, 0, h)),
        pl.BlockSpec((1, nb, pair, ab), lambda b, h, i: (b, 0, h, 0)),
    ]
    return pl.pallas_call(
        kern, grid=(bsz, wq // pair, nb // ATTN_CHUNK), in_specs=in_specs,
        out_specs=pl.BlockSpec((1, tq, pair), lambda b, h, i: (b, i, h)),
        out_shape=jax.ShapeDtypeStruct((bsz, s, wq), BF16),
        scratch_shapes=[pltpu.VMEM((nb, pair), F32), pltpu.VMEM((2, nb, tq), F32)]
        + _flash_scratch(2 * ATTN_CHUNK, HEAD_DIM),
        compiler_params=_params("parallel", "parallel", "arbitrary"), name="moba_attention",
    )(norm_g.reshape(HEAD_DIM, 1), qt, k, vt)


def _out_mlp_kernel(x_ref, ys_ref, yd_ref, ym_ref, mod_ref, g2_ref, wo_ref, w1_ref, w2_ref, fg_ref,
                    o_ref, *, final_norm, ff_chunk):
    x = x_ref[0]
    mod = mod_ref[0]
    ws = ys_ref.shape[2]
    wd = yd_ref.shape[2]
    mix = (jnp.dot(ys_ref[0], wo_ref[0:ws], preferred_element_type=F32)
           + jnp.dot(yd_ref[0], wo_ref[ws:ws + wd], preferred_element_type=F32)
           + jnp.dot(ym_ref[0], wo_ref[ws + wd:], preferred_element_type=F32))
    x = x + mod[2:3] * mix
    h = (_rmsnorm_rows(x, g2_ref[...]) * (1.0 + mod[4:5]) + mod[3:4]).astype(BF16)
    acc = jnp.zeros_like(x)
    for c in range(w1_ref.shape[1] // ff_chunk):
        a = jnp.dot(h, w1_ref[:, c * ff_chunk:(c + 1) * ff_chunk], preferred_element_type=F32)
        a = jnp.square(jnp.maximum(a, 0.0)).astype(BF16)
        acc = acc + jnp.dot(a, w2_ref[c * ff_chunk:(c + 1) * ff_chunk, :], preferred_element_type=F32)
    x = x + mod[5:6] * acc
    if final_norm:
        x = _rmsnorm_rows(x, fg_ref[...])
    o_ref[0] = x


def _out_mlp(x, ys, yd, ym, mod, g2, wo, w1, w2, final_g, final_norm):
    bsz, s, d = x.shape
    tm = TOKEN_TILE
    kern = functools.partial(_out_mlp_kernel, final_norm=final_norm, ff_chunk=d)
    tok = lambda w: pl.BlockSpec((1, tm, w), lambda b, i: (b, i, 0))
    in_specs = [
        tok(d), tok(ys.shape[2]), tok(yd.shape[2]), tok(ym.shape[2]),
        pl.BlockSpec((1, 6, d), lambda b, i: (b, 0, 0)),
        _const_spec((1, d)), _const_spec(wo.shape), _const_spec(w1.shape), _const_spec(w2.shape),
        _const_spec((1, d)),
    ]
    return pl.pallas_call(
        kern, grid=(bsz, s // tm), in_specs=in_specs, out_specs=tok(d),
        out_shape=jax.ShapeDtypeStruct((bsz, s, d), F32),
        compiler_params=_params("parallel", "parallel"), name="out_mlp",
    )(x, ys, yd, ym, mod, g2, wo, w1, w2, final_g)


def kernel(x, c, positions, norm1_g, norm2_g, w_ada, b_ada, w_in, w_out, ssm_a_re, ssm_a_im, ssm_log_dt, ssm_b_re, ssm_b_im, ssm_c_re, ssm_c_im, ssm_d, ssm_glu_w, ssm_glu_b, ssm_norm_g, diff_lq1, diff_lk1, diff_lq2, diff_lk2, diff_subln_g, moba_norm_g, mlp_w1, mlp_w2, final_g):
    bsz, s, d = x.shape
    depth = w_in.shape[0]
    w_ssm = ssm_d.shape[1]
    w_diff = DIFF_HEADS * 2 * HEAD_DIM
    w_moba = MOBA_HEADS * HEAD_DIM
    assert s % TOKEN_TILE == 0 and TOKEN_TILE % ATTN_BLOCK == 0 and s % (ATTN_CHUNK * ATTN_BLOCK) == 0
    assert w_in.shape[2] == w_ssm + 3 * w_diff + 3 * w_moba

    cos_t, sin_t = _rope_tables(positions)
    mod_all = _ada_mod(c, w_ada, b_ada).reshape(depth, bsz, 6, d)

    o_dq = w_ssm
    o_dk = o_dq + w_diff
    o_dv = o_dk + w_diff
    o_mq = o_dv + w_diff
    o_mk = o_mq + w_moba
    o_mv = o_mk + w_moba
    widths = (w_diff, w_diff, w_moba, w_moba, w_diff, w_moba)

    for l in range(depth):
        wl = w_in[l]
        wu = wl[:, :o_dq].astype(BF16)
        wt = jnp.concatenate([wl[:, o_dq:o_dk], wl[:, o_dk:o_dv], wl[:, o_mq:o_mk], wl[:, o_mk:o_mv],
                              wl[:, o_dv:o_mq], wl[:, o_mv:]], axis=1).T.astype(BF16)
        mod = mod_all[l]
        u, dq, dk, dv, mq, mk, mv = _projection(x, mod, norm1_g[l][None], wu, wt, cos_t, sin_t, widths)

        b_blk, c_blk, coef = _ssm_prep(ssm_a_re[l], ssm_a_im[l], ssm_log_dt[l], ssm_b_re[l], ssm_b_im[l],
                                       ssm_c_re[l], ssm_c_im[l])
        y_ssm = _ssm(u, b_blk, c_blk, coef, ssm_d[l][None], ssm_glu_w[l].astype(BF16),
                     ssm_glu_b[l][None], ssm_norm_g[l][None])

        lam_init = 0.8 - 0.6 * math.exp(-0.3 * l)
        lam_params = jnp.stack([diff_lq1[l], diff_lk1[l], diff_lq2[l], diff_lk2[l]]).astype(F32)
        y_diff = _diff_attention(dq, dk, dv, lam_params, diff_subln_g[l], lam_init)
        y_moba = _moba_attention(mq, mk, mv, moba_norm_g[l])

        x = _out_mlp(x, y_ssm, y_diff, y_moba, mod, norm2_g[l][None], w_out[l].astype(BF16),
                     mlp_w1[l].astype(BF16), mlp_w2[l].astype(BF16), final_g[None],
                     final_norm=(l == depth - 1))
    return x
```

```python
import functools
import math

import jax
import jax.numpy as jnp
from jax import lax
from jax.experimental import pallas as pl
from jax.experimental.pallas import tpu as pltpu

F32 = jnp.float32
BF16 = jnp.bfloat16

SSM_GROUP = 16
SSM_STATE = 64
HEAD_DIM = 64
DIFF_HEADS = 4
MOBA_HEADS = 4
MOBA_BLOCK = 256
MOBA_TOPK = 3
ROPE_THETA = 500000.0
ROPE_HALF = 8
EPS = 1e-6
LOG2_E = math.log2(math.e)
NEG = -1e30
BELOW_NEG = -3e38

SUBLANES = 8
TOKEN_TILE = 512
ATTN_BLOCK = MOBA_BLOCK
ATTN_CHUNK = 4
VMEM_LIMIT = 56 * 1024 * 1024


def _params(*sem):
    return pltpu.CompilerParams(dimension_semantics=sem, vmem_limit_bytes=VMEM_LIMIT)


def _const_spec(shape):
    nd = len(shape)
    return pl.BlockSpec(shape, lambda *_: (0,) * nd)


def _rope_kernel(inv_ref, pos_ref, cos_ref, sin_ref):
    ang = inv_ref[...] * pos_ref[0].astype(F32)
    cos_ref[0] = jnp.cos(ang)
    sin_ref[0] = jnp.sin(ang)


def _rope_tables(positions):
    bsz, s = positions.shape
    rot = 2 * ROPE_HALF
    inv = (ROPE_THETA ** (-jnp.arange(0, rot, 2, dtype=F32) / rot)).reshape(ROPE_HALF, 1)
    out = jax.ShapeDtypeStruct((bsz, ROPE_HALF, s), F32)
    return pl.pallas_call(
        _rope_kernel,
        grid=(bsz,),
        in_specs=[_const_spec((ROPE_HALF, 1)),
                  pl.BlockSpec((1, 1, s), lambda b: (b, 0, 0))],
        out_specs=[pl.BlockSpec((1, ROPE_HALF, s), lambda b: (b, 0, 0))] * 2,
        out_shape=[out, out],
        compiler_params=_params("parallel"),
        name="rope_tables",
    )(inv, positions.reshape(bsz, 1, s))


def _ada_kernel(c_ref, w_ref, b_ref, o_ref):
    c = c_ref[...]
    sc = c * jax.nn.sigmoid(c)
    o_ref[0] = jnp.dot(sc, w_ref[0], preferred_element_type=F32,
                       precision=lax.Precision.HIGHEST) + b_ref[0]


def _ada_mod(c, w_ada, b_ada):
    depth, d, d6 = w_ada.shape
    bsz = c.shape[0]
    nchunk = d6 // d
    return pl.pallas_call(
        _ada_kernel,
        grid=(depth, nchunk),
        in_specs=[_const_spec((bsz, d)),
                  pl.BlockSpec((1, d, d), lambda l, j: (l, 0, j)),
                  pl.BlockSpec((1, 1, d), lambda l, j: (l, 0, j))],
        out_specs=pl.BlockSpec((1, bsz, d), lambda l, j: (l, 0, j)),
        out_shape=jax.ShapeDtypeStruct((depth, bsz, d6), F32),
        compiler_params=_params("parallel", "parallel"),
        name="ada_mod",
    )(c, w_ada, b_ada.reshape(depth, 1, d6))


def _ssm_prep_kernel(are_ref, aim_ref, ldt_ref, bre_ref, bim_ref,
                     bbr_ref, bbi_ref, pwr_ref, pwi_ref):
    ar = are_ref[...]
    ai = aim_ref[...]
    dt = jnp.exp(ldt_ref[...])
    mag = jnp.exp(ar * dt)
    abr = mag * jnp.cos(ai * dt)
    abi = mag * jnp.sin(ai * dt)
    den = ar * ar + ai * ai
    nr = abr - 1.0
    ni = abi
    kr = (nr * ar + ni * ai) / den
    ki = (ni * ar - nr * ai) / den
    br = bre_ref[...]
    bi = bim_ref[...]
    bbr_ref[...] = kr[None] * br - ki[None] * bi
    bbi_ref[...] = kr[None] * bi + ki[None] * br
    pr, pi = abr, abi
    pwr_ref[0] = pr
    pwi_ref[0] = pi
    for j in range(1, SUBLANES):
        pr, pi = pr * abr - pi * abi, pr * abi + pi * abr
        pwr_ref[j] = pr
        pwi_ref[j] = pi


def _ssm_prep(a_re, a_im, log_dt, b_re, b_im, c_re, c_im):
    g, p = a_re.shape
    cch = b_re.shape[-1]
    w = g * cch
    n = g * p
    bt = lambda b: jnp.transpose(b, (2, 0, 1))
    shapes = [jax.ShapeDtypeStruct((cch, g, p), F32)] * 2 + [jax.ShapeDtypeStruct((SUBLANES, g, p), F32)] * 2
    bbr, bbi, pwr, pwi = pl.pallas_call(
        _ssm_prep_kernel, out_shape=shapes, name="ssm_prep",
    )(a_re, a_im, log_dt.reshape(g, 1), bt(b_re), bt(b_im))
    eye = jnp.eye(g, dtype=F32)
    blk_b = lambda bb: jnp.einsum('cgp,gh->gchp', bb, eye).reshape(w, n)
    b_blk = jnp.concatenate([blk_b(bbr), blk_b(bbi)], axis=1).astype(BF16)
    blk_c = lambda cc: jnp.einsum('gcp,gh->gphc', cc, eye).reshape(n, w)
    c_blk = jnp.concatenate([blk_c(c_re), -blk_c(c_im)], axis=0).astype(BF16)
    pwr = pwr.reshape(SUBLANES, n)
    pwi = pwi.reshape(SUBLANES, n)
    row = jnp.arange(SUBLANES)[:, None]
    tabs = []
    for shift in (1, 2, 4):
        keep = row >= shift
        tabs.append(jnp.where(keep, pwr[shift - 1][None], 0.0))
        tabs.append(jnp.where(keep, pwi[shift - 1][None], 0.0))
    tabs += [pwr, pwi]
    return b_blk, c_blk, jnp.stack(tabs)


def _rmsnorm_rows(x, g):
    return x * lax.rsqrt(jnp.mean(x * x, axis=-1, keepdims=True) + EPS) * g


def _proj_kernel(x_ref, mod_ref, g_ref, wu_ref, wt_ref, cos_ref, sin_ref,
                 u_ref, dq_ref, dk_ref, dv_ref, mq_ref, mk_ref, mv_ref, *, n_rot, widths):
    wdq, wdk, wmq, wmk, wdv, wmv = widths
    x = x_ref[0]
    mod = mod_ref[0]
    h = _rmsnorm_rows(x, g_ref[...]) * (1.0 + mod[1:2]) + mod[0:1]
    hb = h.astype(BF16)
    u_ref[0] = jnp.dot(hb, wu_ref[...], preferred_element_type=F32)
    pt = lax.dot_general(wt_ref[...], hb, (((1,), (1,)), ((), ())), preferred_element_type=F32)
    tm = pt.shape[1]
    rot = pt[:n_rot].reshape(n_rot // HEAD_DIM, HEAD_DIM, tm)
    cos = cos_ref[0][None]
    sin = sin_ref[0][None]
    x1 = rot[:, 0:ROPE_HALF]
    x2 = rot[:, ROPE_HALF:2 * ROPE_HALF]
    rot = jnp.concatenate([x1 * cos - x2 * sin, x2 * cos + x1 * sin, rot[:, 2 * ROPE_HALF:]], axis=1)
    rot = rot.reshape(n_rot, tm)
    qscale = HEAD_DIM ** -0.5 * LOG2_E
    o = 0
    dq = (rot[o:o + wdq] * qscale).astype(BF16); o += wdq
    dk_ref[0] = rot[o:o + wdk].T.astype(BF16); o += wdk
    mq = (rot[o:o + wmq] * qscale).astype(BF16); o += wmq
    mk_ref[0] = rot[o:o + wmk].T.astype(BF16); o += wmk
    dv = pt[o:o + wdv].astype(BF16); o += wdv
    mv = pt[o:o + wmv].astype(BF16)
    for j in range(tm // ATTN_BLOCK):
        blk = slice(j * ATTN_BLOCK, (j + 1) * ATTN_BLOCK)
        dq_ref[0, j] = dq[:, blk]
        dv_ref[0, j] = dv[:, blk]
        mq_ref[0, j] = mq[:, blk]
        mv_ref[0, j] = mv[:, blk]


def _projection(x, mod, g, wu, wt, cos_t, sin_t, widths):
    bsz, s, d = x.shape
    tm = TOKEN_TILE
    ns = s // tm
    wdq, wdk, wmq, wmk, wdv, wmv = widths
    n_rot = wdq + wdk + wmq + wmk
    ab = ATTN_BLOCK
    nab = tm // ab
    kern = functools.partial(_proj_kernel, n_rot=n_rot, widths=widths)
    tiles_t = lambda w: jax.ShapeDtypeStruct((bsz, s // ab, w, ab), BF16)
    out_shape = [
        jax.ShapeDtypeStruct((bsz, s, wu.shape[1]), F32),
        tiles_t(wdq),
        jax.ShapeDtypeStruct((bsz, s, wdk), BF16),
        tiles_t(wdv),
        tiles_t(wmq),
        jax.ShapeDtypeStruct((bsz, s, wmk), BF16),
        tiles_t(wmv),
    ]
    spec_t = lambda w: pl.BlockSpec((1, nab, w, ab), lambda b, i: (b, i, 0, 0))
    out_specs = [
        pl.BlockSpec((1, tm, wu.shape[1]), lambda b, i: (b, i, 0)),
        spec_t(wdq),
        pl.BlockSpec((1, tm, wdk), lambda b, i: (b, i, 0)),
        spec_t(wdv),
        spec_t(wmq),
        pl.BlockSpec((1, tm, wmk), lambda b, i: (b, i, 0)),
        spec_t(wmv),
    ]
    in_specs = [
        pl.BlockSpec((1, tm, d), lambda b, i: (b, i, 0)),
        pl.BlockSpec((1, 6, d), lambda b, i: (b, 0, 0)),
        _const_spec((1, d)),
        _const_spec(wu.shape),
        _const_spec(wt.shape),
        pl.BlockSpec((1, ROPE_HALF, tm), lambda b, i: (b, 0, i)),
        pl.BlockSpec((1, ROPE_HALF, tm), lambda b, i: (b, 0, i)),
    ]
    return pl.pallas_call(
        kern, grid=(bsz, ns), in_specs=in_specs, out_specs=out_specs, out_shape=out_shape,
        compiler_params=_params("parallel", "parallel"), name="projection",
    )(x, mod, g, wu, wt, cos_t, sin_t)


def _gelu_tanh(y):
    return 0.5 * y * (1.0 + jnp.tanh(math.sqrt(2.0 / math.pi) * (y + 0.044715 * (y * y * y))))


def _ssm_kernel(u_ref, bblk_ref, cblk_ref, coef_ref, d_ref, gw_ref, gb_ref, ng_ref,
                y_ref, bu_ref, carry_ref, *, n_state):
    n = n_state

    @pl.when(pl.program_id(1) == 0)
    def _():
        carry_ref[...] = jnp.zeros_like(carry_ref)

    u = u_ref[0]
    tm = u.shape[0]
    bu_ref[...] = jnp.dot(u.astype(BF16), bblk_ref[...], preferred_element_type=F32)

    def cmul_add(xr, xi, ar, ai, sr, si):
        return xr + (ar * sr - ai * si), xi + (ar * si + ai * sr)

    def group(r, carry):
        cr, ci = carry
        rows = pl.ds(pl.multiple_of(r * SUBLANES, SUBLANES), SUBLANES)
        xr = bu_ref[rows, 0:n]
        xi = bu_ref[rows, n:2 * n]
        for lvl, shift in enumerate((1, 2, 4)):
            sr = pltpu.roll(xr, shift, axis=0)
            si = pltpu.roll(xi, shift, axis=0)
            xr, xi = cmul_add(xr, xi, coef_ref[2 * lvl], coef_ref[2 * lvl + 1], sr, si)
        xr, xi = cmul_add(xr, xi, coef_ref[6], coef_ref[7], cr, ci)
        bu_ref[rows, 0:n] = xr
        bu_ref[rows, n:2 * n] = xi
        last = SUBLANES - 1
        return (jnp.broadcast_to(xr[last:last + 1], xr.shape),
                jnp.broadcast_to(xi[last:last + 1], xi.shape))

    cr, ci = lax.fori_loop(0, tm // SUBLANES, group, (carry_ref[0], carry_ref[1]))
    carry_ref[0] = cr
    carry_ref[1] = ci

    y = jnp.dot(bu_ref[...].astype(BF16), cblk_ref[...], preferred_element_type=F32)
    y = y + d_ref[...] * u
    y = _gelu_tanh(y)
    z = jnp.dot(y.astype(BF16), gw_ref[...], preferred_element_type=F32) + gb_ref[...]
    y = y * jax.nn.sigmoid(z)
    y_ref[0] = _rmsnorm_rows(y, ng_ref[...]).astype(y_ref.dtype)


def _ssm(u, b_blk, c_blk, coef, d_skip, glu_w, glu_b, norm_g):
    bsz, s, w = u.shape
    n = coef.shape[-1]
    tm = TOKEN_TILE
    kern = functools.partial(_ssm_kernel, n_state=n)
    in_specs = [
        pl.BlockSpec((1, tm, w), lambda b, i: (b, i, 0)),
        _const_spec(b_blk.shape), _const_spec(c_blk.shape), _const_spec(coef.shape),
        _const_spec((1, w)), _const_spec((w, w)), _const_spec((1, w)), _const_spec((1, w)),
    ]
    return pl.pallas_call(
        kern, grid=(bsz, s // tm), in_specs=in_specs,
        out_specs=pl.BlockSpec((1, tm, w), lambda b, i: (b, i, 0)),
        out_shape=jax.ShapeDtypeStruct((bsz, s, w), BF16),
        scratch_shapes=[pltpu.VMEM((tm, 2 * n), F32), pltpu.VMEM((2, SUBLANES, n), F32)],
        compiler_params=_params("parallel", "arbitrary"), name="ssm",
    )(u, b_blk, c_blk, coef, d_skip, glu_w, glu_b, norm_g)


def _row_groups(x):
    rows, lanes = x.shape
    return x.reshape(4, rows // (4 * SUBLANES), SUBLANES, lanes)


def _score_stage(k_ref, first_blk, n_gran, qz, tile_bias, s_ref, mx_ref):
    ab = ATTN_BLOCK
    for w in range(n_gran):
        kb = k_ref[0, pl.ds(pl.multiple_of((first_blk + w) * ab, ab), ab), :]
        s = jnp.dot(kb, qz, preferred_element_type=F32)
        tb = tile_bias(w)
        if tb is not None:
            s = s + tb
        s_ref[w * ab:(w + 1) * ab, :] = s
        mx_ref[w] = jnp.max(jnp.max(_row_groups(s), axis=1), axis=0)


def _softmax_stage(s_ref, mx_ref, vt_ref, first_blk, n_gran, v_rows, row_bias, m, l, acc_ref):
    ab = ATTN_BLOCK
    rbs = [row_bias(w) for w in range(n_gran)]
    raw = mx_ref[0]
    seen = raw if rbs[0] is None else raw + rbs[0]
    for w in range(1, n_gran):
        raw = jnp.maximum(raw, mx_ref[w])
        seen = jnp.maximum(seen, mx_ref[w] if rbs[w] is None else mx_ref[w] + rbs[w])
    m_new = jnp.maximum(m, jnp.max(seen, axis=0, keepdims=True))
    shift = jnp.where(m_new > 0.5 * NEG, m_new, jnp.max(raw, axis=0, keepdims=True))
    alpha = jnp.exp2(m - m_new)
    psum = None
    pv = None
    for w in range(n_gran):
        p = jnp.exp2(s_ref[w * ab:(w + 1) * ab, :] - (shift if rbs[w] is None else shift - rbs[w]))
        part = jnp.sum(jnp.sum(_row_groups(p), axis=1), axis=0)
        psum = part if psum is None else psum + part
        d = jnp.dot(vt_ref[0, first_blk + w, v_rows, :], p.astype(BF16), preferred_element_type=F32)
        pv = d if pv is None else pv + d
    acc_ref[...] = alpha * acc_ref[...] + pv
    return m_new, alpha * l + jnp.sum(psum, axis=0, keepdims=True)


def _flash_tile(g, k_ref, vt_ref, chains, full_bias, diag_bias, qz_ref, s_ref, mx_ref, acc_ref):
    nc = len(chains)
    t = s_ref.shape[-1]
    acc_ref[...] = jnp.zeros(acc_ref.shape, F32)
    neg = jnp.full((1, t), NEG, F32)
    zero = jnp.zeros((1, t), F32)

    def score(c, first, n_gran, bias):
        _score_stage(k_ref, first, n_gran, qz_ref[c], bias[0], s_ref.at[c], mx_ref.at[c])

    def softmax(c, first, n_gran, bias, m, l):
        return _softmax_stage(s_ref.at[c], mx_ref.at[c], vt_ref, first, n_gran, chains[c][1], bias[1],
                              m, l, acc_ref.at[c])

    score(0, 0, ATTN_CHUNK, full_bias(0, 0))

    def step(i, carry):
        first = i * ATTN_CHUNK
        out = []
        for c in range(nc):
            if c + 1 < nc:
                score(c + 1, first, ATTN_CHUNK, full_bias(c + 1, first))
            else:
                score(0, first + ATTN_CHUNK, ATTN_CHUNK, full_bias(0, first + ATTN_CHUNK))
            out += softmax(c, first, ATTN_CHUNK, full_bias(c, first), carry[2 * c], carry[2 * c + 1])
        return tuple(out)

    carry = lax.fori_loop(0, g, step, (neg, zero) * nc)
    first = g * ATTN_CHUNK
    out = []
    score(0, first, chains[0][0] + 1, diag_bias(0))
    for c in range(nc):
        if c + 1 < nc:
            score(c + 1, first, chains[c + 1][0] + 1, diag_bias(c + 1))
        out.append(softmax(c, first, chains[c][0] + 1, diag_bias(c), carry[2 * c], carry[2 * c + 1]))
    return out


def _flash_scratch(nc, dv):
    ab = ATTN_BLOCK
    return [pltpu.VMEM((nc, 2 * HEAD_DIM, ab), BF16),
            pltpu.VMEM((nc, ATTN_CHUNK * ab, ab), F32),
            pltpu.VMEM((nc, ATTN_CHUNK, SUBLANES, ab), F32),
            pltpu.VMEM((nc, dv, ab), F32),
            pltpu.VMEM((ab, ab), F32)]


def _causal_bias(t):
    key = lax.broadcasted_iota(jnp.int32, (t, t), 0)
    qry = lax.broadcasted_iota(jnp.int32, (t, t), 1)
    return jnp.where(key <= qry, 0.0, NEG)


def _diff_kernel(lam_ref, g_ref, qt_ref, k_ref, vt_ref, o_ref, qz_ref, s_ref, mx_ref, acc_ref, tri_ref,
                 *, lam_init):
    g = pl.program_id(2)
    ab = ATTN_BLOCK
    chains = [(qb, slice(None)) for qb in range(ATTN_CHUNK) for _ in range(2)]
    tri_ref[...] = _causal_bias(ab)
    row = lax.broadcasted_iota(jnp.int32, (2 * HEAD_DIM, ab), 0)
    for qb in range(ATTN_CHUNK):
        qt = qt_ref[0, qb].astype(F32)
        qz_ref[2 * qb] = jnp.where(row < HEAD_DIM, qt, 0.0).astype(BF16)
        qz_ref[2 * qb + 1] = jnp.where(row >= HEAD_DIM, qt, 0.0).astype(BF16)

    none = lambda w: None

    def diag_bias(c):
        qb = chains[c][0]
        return (lambda w: tri_ref[...] if w == qb else None), none

    ml = _flash_tile(g, k_ref, vt_ref, chains, lambda c, first: (none, none), diag_bias,
                     qz_ref, s_ref, mx_ref, acc_ref)

    lp = lam_ref[...]
    lam = (jnp.exp(jnp.sum(lp[0:1] * lp[1:2], axis=1, keepdims=True))
           - jnp.exp(jnp.sum(lp[2:3] * lp[3:4], axis=1, keepdims=True)) + lam_init)
    for qb in range(ATTN_CHUNK):
        o = acc_ref[2 * qb] / ml[2 * qb][1] - lam * (acc_ref[2 * qb + 1] / ml[2 * qb + 1][1])
        o = o * lax.rsqrt(jnp.mean(o * o, axis=0, keepdims=True) + EPS) * g_ref[...]
        o_ref[0, qb * ab:(qb + 1) * ab, :] = (o * (1.0 - lam_init)).T.astype(o_ref.dtype)


def _diff_attention(qt, k, vt, lam_params, subln_g, lam_init):
    bsz, nb, wq, ab = qt.shape
    s = k.shape[1]
    dv = 2 * HEAD_DIM
    nh = wq // dv
    tq = ATTN_CHUNK * ab
    kern = functools.partial(_diff_kernel, lam_init=lam_init)
    in_specs = [
        _const_spec(lam_params.shape),
        _const_spec((dv, 1)),
        pl.BlockSpec((1, ATTN_CHUNK, dv, ab), lambda b, h, i: (b, i, h, 0)),
        pl.BlockSpec((1, s, dv), lambda b, h, i: (b, 0, h)),
        pl.BlockSpec((1, nb, dv, ab), lambda b, h, i: (b, 0, h, 0)),
    ]
    return pl.pallas_call(
        kern, grid=(bsz, nh, nb // ATTN_CHUNK), in_specs=in_specs,
        out_specs=pl.BlockSpec((1, tq, dv), lambda b, h, i: (b, i, h)),
        out_shape=jax.ShapeDtypeStruct((bsz, s, wq), BF16),
        scratch_shapes=_flash_scratch(2 * ATTN_CHUNK, dv),
        compiler_params=_params("parallel", "parallel", "arbitrary"), name="diff_attention",
    )(lam_params, subln_g.reshape(dv, 1), qt, k, vt)


def _moba_kernel(g_ref, qt_ref, k_ref, vt_ref, o_ref, kmean_ref, bias_ref, qz_ref, s_ref, mx_ref, acc_ref,
                 tri_ref, *, n_blocks):
    g = pl.program_id(2)
    ab = ATTN_BLOCK
    tq = ATTN_CHUNK * ab
    pair = 2 * HEAD_DIM

    @pl.when(g == 0)
    def _():
        def mean_block(j, c):
            kb = k_ref[0, pl.ds(pl.multiple_of(j * ab, ab), ab), :].astype(F32)
            kmean_ref[pl.ds(j, 1), :] = jnp.mean(kb, axis=0, keepdims=True)
            return c
        lax.fori_loop(0, n_blocks, mean_block, 0)

    tri_ref[...] = _causal_bias(ab)
    chains = [(qb, slice(hh * HEAD_DIM, (hh + 1) * HEAD_DIM)) for qb in range(ATTN_CHUNK) for hh in range(2)]
    qt = jnp.concatenate([qt_ref[0, qb] for qb in range(ATTN_CHUNK)], axis=1).astype(F32)
    row = lax.broadcasted_iota(jnp.int32, (pair, tq), 0)
    lane = lax.broadcasted_iota(jnp.int32, (n_blocks, pair), 1)
    bid = lax.broadcasted_iota(jnp.int32, (n_blocks, tq), 0)
    qblk = g * ATTN_CHUNK + lax.broadcasted_iota(jnp.int32, (n_blocks, tq), 1) // ab
    past = bid < qblk
    bid = bid.astype(F32)
    kmean = kmean_ref[...]
    for hh in range(2):
        qf = jnp.where((row >= hh * HEAD_DIM) & (row < (hh + 1) * HEAD_DIM), qt, 0.0)
        for qb in range(ATTN_CHUNK):
            qz_ref[2 * qb + hh] = qf[:, qb * ab:(qb + 1) * ab].astype(BF16)
        km = jnp.where((lane >= hh * HEAD_DIM) & (lane < (hh + 1) * HEAD_DIM), kmean, 0.0)
        gate = jnp.dot(km, qf, preferred_element_type=F32, precision=lax.Precision.HIGHEST)
        gate = jnp.where(past, gate, NEG)
        sel = jnp.zeros_like(gate)
        for _ in range(MOBA_TOPK):
            top = jnp.max(gate, axis=0, keepdims=True)
            first = jnp.min(jnp.where(gate == top, bid, float(n_blocks)), axis=0, keepdims=True)
            hit = bid == first
            sel = jnp.where(hit, 1.0, sel)
            gate = jnp.where(hit, BELOW_NEG, gate)
        bias_ref[hh] = jnp.where(past, jnp.where(sel > 0.0, 0.0, NEG), NEG)

    none = lambda w: None

    def picked(c, blk):
        qb, hh = c // 2, c % 2
        return bias_ref[hh, pl.ds(blk, 1), qb * ab:(qb + 1) * ab]

    def full_bias(c, first):
        return none, lambda w: picked(c, first + w)

    def diag_bias(c):
        qb = chains[c][0]
        return ((lambda w: tri_ref[...] if w == qb else None),
                (lambda w: None if w == qb else picked(c, g * ATTN_CHUNK + w)))

    ml = _flash_tile(g, k_ref, vt_ref, chains, full_bias, diag_bias, qz_ref, s_ref, mx_ref, acc_ref)

    for qb in range(ATTN_CHUNK):
        outs = []
        for hh in range(2):
            o = acc_ref[2 * qb + hh] / ml[2 * qb + hh][1]
            outs.append(o * lax.rsqrt(jnp.mean(o * o, axis=0, keepdims=True) + EPS) * g_ref[...])
        o_ref[0, qb * ab:(qb + 1) * ab, :] = jnp.concatenate(outs, axis=0).T.astype(o_ref.dtype)


def _moba_attention(qt, k, vt, norm_g):
    bsz, nb, wq, ab = qt.shape
    s = k.shape[1]
    pair = 2 * HEAD_DIM
    tq = ATTN_CHUNK * ab
    kern = functools.partial(_moba_kernel, n_blocks=nb)
    in_specs = [
        _const_spec((HEAD_DIM, 1)),
        pl.BlockSpec((1, ATTN_CHUNK, pair, ab), lambda b, h, i: (b, i, h, 0)),
        pl.BlockSpec((1, s, pair), lambda b, h, i: (b, 0, h)),
        pl.BlockSpec((1, nb, pair, ab), lambda b, h, i: (b, 0, h, 0)),
    ]
    return pl.pallas_call(
        kern, grid=(bsz, wq // pair, nb // ATTN_CHUNK), in_specs=in_specs,
        out_specs=pl.BlockSpec((1, tq, pair), lambda b, h, i: (b, i, h)),
        out_shape=jax.ShapeDtypeStruct((bsz, s, wq), BF16),
        scratch_shapes=[pltpu.VMEM((nb, pair), F32), pltpu.VMEM((2, nb, tq), F32)]
        + _flash_scratch(2 * ATTN_CHUNK, HEAD_DIM),
        compiler_params=_params("parallel", "parallel", "arbitrary"), name="moba_attention",
    )(norm_g.reshape(HEAD_DIM, 1), qt, k, vt)


def _out_mlp_kernel(x_ref, ys_ref, yd_ref, ym_ref, mod_ref, g2_ref, wo_ref, w1_ref, w2_ref, fg_ref,
                    o_ref, *, final_norm, ff_chunk):
    x = x_ref[0]
    mod = mod_ref[0]
    ws = ys_ref.shape[2]
    wd = yd_ref.shape[2]
    mix = (jnp.dot(ys_ref[0], wo_ref[0:ws], preferred_element_type=F32)
           + jnp.dot(yd_ref[0], wo_ref[ws:ws + wd], preferred_element_type=F32)
           + jnp.dot(ym_ref[0], wo_ref[ws + wd:], preferred_element_type=F32))
    x = x + mod[2:3] * mix
    h = (_rmsnorm_rows(x, g2_ref[...]) * (1.0 + mod[4:5]) + mod[3:4]).astype(BF16)
    acc = jnp.zeros_like(x)
    for c in range(w1_ref.shape[1] // ff_chunk):
        a = jnp.dot(h, w1_ref[:, c * ff_chunk:(c + 1) * ff_chunk], preferred_element_type=F32)
        a = jnp.square(jnp.maximum(a, 0.0)).astype(BF16)
        acc = acc + jnp.dot(a, w2_ref[c * ff_chunk:(c + 1) * ff_chunk, :], preferred_element_type=F32)
    x = x + mod[5:6] * acc
    if final_norm:
        x = _rmsnorm_rows(x, fg_ref[...])
    o_ref[0] = x


def _out_mlp(x, ys, yd, ym, mod, g2, wo, w1, w2, final_g, final_norm):
    bsz, s, d = x.shape
    tm = TOKEN_TILE
    kern = functools.partial(_out_mlp_kernel, final_norm=final_norm, ff_chunk=d)
    tok = lambda w: pl.BlockSpec((1, tm, w), lambda b, i: (b, i, 0))
    in_specs = [
        tok(d), tok(ys.shape[2]), tok(yd.shape[2]), tok(ym.shape[2]),
        pl.BlockSpec((1, 6, d), lambda b, i: (b, 0, 0)),
        _const_spec((1, d)), _const_spec(wo.shape), _const_spec(w1.shape), _const_spec(w2.shape),
        _const_spec((1, d)),
    ]
    return pl.pallas_call(
        kern, grid=(bsz, s // tm), in_specs=in_specs, out_specs=tok(d),
        out_shape=jax.ShapeDtypeStruct((bsz, s, d), F32),
        compiler_params=_params("parallel", "parallel"), name="out_mlp",
    )(x, ys, yd, ym, mod, g2, wo, w1, w2, final_g)


def kernel(x, c, positions, norm1_g, norm2_g, w_ada, b_ada, w_in, w_out, ssm_a_re, ssm_a_im, ssm_log_dt, ssm_b_re, ssm_b_im, ssm_c_re, ssm_c_im, ssm_d, ssm_glu_w, ssm_glu_b, ssm_norm_g, diff_lq1, diff_lk1, diff_lq2, diff_lk2, diff_subln_g, moba_norm_g, mlp_w1, mlp_w2, final_g):
    bsz, s, d = x.shape
    depth = w_in.shape[0]
    w_ssm = ssm_d.shape[1]
    w_diff = DIFF_HEADS * 2 * HEAD_DIM
    w_moba = MOBA_HEADS * HEAD_DIM
    assert s % TOKEN_TILE == 0 and TOKEN_TILE % ATTN_BLOCK == 0 and s % (ATTN_CHUNK * ATTN_BLOCK) == 0
    assert w_in.shape[2] == w_ssm + 3 * w_diff + 3 * w_moba

    cos_t, sin_t = _rope_tables(positions)
    mod_all = _ada_mod(c, w_ada, b_ada).reshape(depth, bsz, 6, d)

    o_dq = w_ssm
    o_dk = o_dq + w_diff
    o_dv = o_dk + w_diff
    o_mq = o_dv + w_diff
    o_mk = o_mq + w_moba
    o_mv = o_mk + w_moba
    widths = (w_diff, w_diff, w_moba, w_moba, w_diff, w_moba)

    for l in range(depth):
        wl = w_in[l]
        wu = wl[:, :o_dq].astype(BF16)
        wt = jnp.concatenate([wl[:, o_dq:o_dk], wl[:, o_dk:o_dv], wl[:, o_mq:o_mk], wl[:, o_mk:o_mv],
                              wl[:, o_dv:o_mq], wl[:, o_mv:]], axis=1).T.astype(BF16)
        mod = mod_all[l]
        u, dq, dk, dv, mq, mk, mv = _projection(x, mod, norm1_g[l][None], wu, wt, cos_t, sin_t, widths)

        b_blk, c_blk, coef = _ssm_prep(ssm_a_re[l], ssm_a_im[l], ssm_log_dt[l], ssm_b_re[l], ssm_b_im[l],
                                       ssm_c_re[l], ssm_c_im[l])
        y_ssm = _ssm(u, b_blk, c_blk, coef, ssm_d[l][None], ssm_glu_w[l].astype(BF16),
                     ssm_glu_b[l][None], ssm_norm_g[l][None])

        lam_init = 0.8 - 0.6 * math.exp(-0.3 * l)
        lam_params = jnp.stack([diff_lq1[l], diff_lk1[l], diff_lq2[l], diff_lk2[l]]).astype(F32)
        y_diff = _diff_attention(dq, dk, dv, lam_params, diff_subln_g[l], lam_init)
        y_moba = _moba_attention(mq, mk, mv, moba_norm_g[l])

        x = _out_mlp(x, y_ssm, y_diff, y_moba, mod, norm2_g[l][None], w_out[l].astype(BF16),
                     mlp_w1[l].astype(BF16), mlp_w2[l].astype(BF16), final_g[None],
                     final_norm=(l == depth - 1))
    return x
```

```python
import functools
import math

import jax
import jax.numpy as jnp
from jax import lax
from jax.experimental import pallas as pl
from jax.experimental.pallas import tpu as pltpu

F32 = jnp.float32
BF16 = jnp.bfloat16

SSM_GROUP = 16
SSM_STATE = 64
HEAD_DIM = 64
DIFF_HEADS = 4
MOBA_HEADS = 4
MOBA_BLOCK = 256
MOBA_TOPK = 3
ROPE_THETA = 500000.0
ROPE_HALF = 8
EPS = 1e-6
LOG2_E = math.log2(math.e)
NEG = -1e30
BELOW_NEG = -3e38

SUBLANES = 8
LANES = 128
TOKEN_TILE = 512
SSM_CHUNK = 4
SSM_TILE = 2048
ATTN_BLOCK = MOBA_BLOCK
ATTN_CHUNK = 4
VMEM_LIMIT = 56 * 1024 * 1024


def _params(*sem):
    return pltpu.CompilerParams(dimension_semantics=sem, vmem_limit_bytes=VMEM_LIMIT)


def _const_spec(shape):
    nd = len(shape)
    return pl.BlockSpec(shape, lambda *_: (0,) * nd)


def _rope_kernel(inv_ref, pos_ref, cos_ref, sin_ref):
    ang = inv_ref[...] * pos_ref[0].astype(F32)
    cos_ref[0] = jnp.cos(ang)
    sin_ref[0] = jnp.sin(ang)


def _rope_tables(positions):
    bsz, s = positions.shape
    rot = 2 * ROPE_HALF
    inv = (ROPE_THETA ** (-jnp.arange(0, rot, 2, dtype=F32) / rot)).reshape(ROPE_HALF, 1)
    out = jax.ShapeDtypeStruct((bsz, ROPE_HALF, s), F32)
    return pl.pallas_call(
        _rope_kernel,
        grid=(bsz,),
        in_specs=[_const_spec((ROPE_HALF, 1)),
                  pl.BlockSpec((1, 1, s), lambda b: (b, 0, 0))],
        out_specs=[pl.BlockSpec((1, ROPE_HALF, s), lambda b: (b, 0, 0))] * 2,
        out_shape=[out, out],
        compiler_params=_params("parallel"),
        name="rope_tables",
    )(inv, positions.reshape(bsz, 1, s))


def _ada_kernel(c_ref, w_ref, b_ref, o_ref):
    c = c_ref[...]
    sc = c * jax.nn.sigmoid(c)
    o_ref[0] = jnp.dot(sc, w_ref[0], preferred_element_type=F32,
                       precision=lax.Precision.HIGHEST) + b_ref[0]


def _ada_mod(c, w_ada, b_ada):
    depth, d, d6 = w_ada.shape
    bsz = c.shape[0]
    nchunk = d6 // d
    return pl.pallas_call(
        _ada_kernel,
        grid=(depth, nchunk),
        in_specs=[_const_spec((bsz, d)),
                  pl.BlockSpec((1, d, d), lambda l, j: (l, 0, j)),
                  pl.BlockSpec((1, 1, d), lambda l, j: (l, 0, j))],
        out_specs=pl.BlockSpec((1, bsz, d), lambda l, j: (l, 0, j)),
        out_shape=jax.ShapeDtypeStruct((depth, bsz, d6), F32),
        compiler_params=_params("parallel", "parallel"),
        name="ada_mod",
    )(c, w_ada, b_ada.reshape(depth, 1, d6))


def _ssm_prep_kernel(are_ref, aim_ref, ldt_ref, bre_ref, bim_ref,
                     bbr_ref, bbi_ref, p1r_ref, p1i_ref, pcr_ref, pci_ref):
    ar = are_ref[...]
    ai = aim_ref[...]
    dt = jnp.exp(ldt_ref[...])
    mag = jnp.exp(ar * dt)
    abr = mag * jnp.cos(ai * dt)
    abi = mag * jnp.sin(ai * dt)
    den = ar * ar + ai * ai
    nr = abr - 1.0
    ni = abi
    kr = (nr * ar + ni * ai) / den
    ki = (ni * ar - nr * ai) / den
    br = bre_ref[...]
    bi = bim_ref[...]
    bbr_ref[...] = kr[None] * br - ki[None] * bi
    bbi_ref[...] = kr[None] * bi + ki[None] * br
    pr, pi = abr, abi
    p1r_ref[0] = pr
    p1i_ref[0] = pi
    for j in range(1, SSM_CHUNK):
        pr, pi = pr * abr - pi * abi, pr * abi + pi * abr
        p1r_ref[j] = pr
        p1i_ref[j] = pi
    cr, ci = pr, pi
    pcr_ref[0] = cr
    pci_ref[0] = ci
    for j in range(1, SUBLANES):
        pr, pi = pr * cr - pi * ci, pr * ci + pi * cr
        pcr_ref[j] = pr
        pci_ref[j] = pi


def _ssm_prep(a_re, a_im, log_dt, b_re, b_im, c_re, c_im):
    g, p = a_re.shape
    cch = b_re.shape[-1]
    w = g * cch
    n = g * p
    nl = SSM_CHUNK
    hi = lax.Precision.HIGHEST
    bt = lambda b: jnp.transpose(b, (2, 0, 1))
    shapes = ([jax.ShapeDtypeStruct((cch, g, p), F32)] * 2 + [jax.ShapeDtypeStruct((nl, g, p), F32)] * 2
              + [jax.ShapeDtypeStruct((SUBLANES, g, p), F32)] * 2)
    bbr, bbi, p1r, p1i, pcr, pci = pl.pallas_call(
        _ssm_prep_kernel, out_shape=shapes, name="ssm_prep",
    )(a_re, a_im, log_dt.reshape(g, 1), bt(b_re), bt(b_im))
    eye = jnp.eye(g, dtype=F32)
    ar = jnp.concatenate([jnp.ones((1, g, p), F32), p1r], axis=0)
    ai = jnp.concatenate([jnp.zeros((1, g, p), F32), p1i], axis=0)
    car = c_re[None] * ar[:, :, None, :] - c_im[None] * ai[:, :, None, :]
    cai = c_re[None] * ai[:, :, None, :] + c_im[None] * ar[:, :, None, :]
    kk = (jnp.einsum('tgcp,dgp->tgcd', car[:nl], bbr, precision=hi)
          - jnp.einsum('tgcp,dgp->tgcd', cai[:nl], bbi, precision=hi))
    lag = jnp.arange(nl)[None, :] - jnp.arange(nl)[:, None]
    ksel = jnp.where((lag >= 0)[:, :, None, None, None], kk[jnp.clip(lag, 0, nl - 1)], 0.0)
    w1 = jnp.einsum('stgcd,gh->sgdthc', ksel, eye).reshape(nl * w, nl * w)
    rev_r, rev_i = ar[nl - 1::-1][:nl], ai[nl - 1::-1][:nl]
    abr_ = rev_r[:, None] * bbr[None] - rev_i[:, None] * bbi[None]
    abi_ = rev_r[:, None] * bbi[None] + rev_i[:, None] * bbr[None]
    blk2 = lambda x: jnp.einsum('tdgp,gh->tgdhp', x, eye).reshape(nl * w, n)
    w2 = jnp.concatenate([blk2(abr_), blk2(abi_)], axis=1)
    blk3 = lambda x: jnp.einsum('tgcp,gh->gpthc', x, eye).reshape(n, nl * w)
    w3 = jnp.concatenate([blk3(car[1:]), -blk3(cai[1:])], axis=0)
    pcr = pcr.reshape(SUBLANES, n)
    pci = pci.reshape(SUBLANES, n)
    row = jnp.arange(SUBLANES)[:, None]
    tabs = []
    for shift in (1, 2, 4):
        keep = row >= shift
        tabs.append(jnp.where(keep, pcr[shift - 1][None], 0.0))
        tabs.append(jnp.where(keep, pci[shift - 1][None], 0.0))
    tabs += [pcr, pci]
    return w1.astype(BF16), w2.astype(BF16), w3.astype(BF16), jnp.stack(tabs)


def _rmsnorm_rows(x, g):
    return x * lax.rsqrt(jnp.mean(x * x, axis=-1, keepdims=True) + EPS) * g


def _proj_kernel(x_ref, mod_ref, g_ref, wu_ref, wt_ref, cos_ref, sin_ref,
                 u_ref, dq_ref, dk_ref, dv_ref, mq_ref, mk_ref, mv_ref, *, n_rot, widths):
    wdq, wdk, wmq, wmk, wdv, wmv = widths
    x = x_ref[0]
    mod = mod_ref[0]
    h = _rmsnorm_rows(x, g_ref[...]) * (1.0 + mod[1:2]) + mod[0:1]
    hb = h.astype(BF16)
    u = jnp.dot(hb, wu_ref[...], preferred_element_type=F32)
    for j in range(u_ref.shape[1]):
        u_ref[0, j] = u[:, j * LANES:(j + 1) * LANES]
    pt = lax.dot_general(wt_ref[...], hb, (((1,), (1,)), ((), ())), preferred_element_type=F32)
    tm = pt.shape[1]
    rot = pt[:n_rot].reshape(n_rot // HEAD_DIM, HEAD_DIM, tm)
    cos = cos_ref[0][None]
    sin = sin_ref[0][None]
    x1 = rot[:, 0:ROPE_HALF]
    x2 = rot[:, ROPE_HALF:2 * ROPE_HALF]
    rot = jnp.concatenate([x1 * cos - x2 * sin, x2 * cos + x1 * sin, rot[:, 2 * ROPE_HALF:]], axis=1)
    rot = rot.reshape(n_rot, tm)
    qscale = HEAD_DIM ** -0.5 * LOG2_E
    o = 0
    dq = (rot[o:o + wdq] * qscale).astype(BF16); o += wdq
    dk_ref[0] = rot[o:o + wdk].T.astype(BF16); o += wdk
    mq = (rot[o:o + wmq] * qscale).astype(BF16); o += wmq
    mk_ref[0] = rot[o:o + wmk].T.astype(BF16); o += wmk
    dv = pt[o:o + wdv].astype(BF16); o += wdv
    mv = pt[o:o + wmv].astype(BF16)
    for j in range(tm // ATTN_BLOCK):
        blk = slice(j * ATTN_BLOCK, (j + 1) * ATTN_BLOCK)
        dq_ref[0, j] = dq[:, blk]
        dv_ref[0, j] = dv[:, blk]
        mq_ref[0, j] = mq[:, blk]
        mv_ref[0, j] = mv[:, blk]


def _projection(x, mod, g, wu, wt, cos_t, sin_t, widths):
    bsz, s, d = x.shape
    tm = TOKEN_TILE
    ns = s // tm
    wdq, wdk, wmq, wmk, wdv, wmv = widths
    n_rot = wdq + wdk + wmq + wmk
    ab = ATTN_BLOCK
    nab = tm // ab
    kern = functools.partial(_proj_kernel, n_rot=n_rot, widths=widths)
    tiles_t = lambda w: jax.ShapeDtypeStruct((bsz, s // ab, w, ab), BF16)
    out_shape = [
        jax.ShapeDtypeStruct((bsz, wu.shape[1] // LANES, s, LANES), F32),
        tiles_t(wdq),
        jax.ShapeDtypeStruct((bsz, s, wdk), BF16),
        tiles_t(wdv),
        tiles_t(wmq),
        jax.ShapeDtypeStruct((bsz, s, wmk), BF16),
        tiles_t(wmv),
    ]
    spec_t = lambda w: pl.BlockSpec((1, nab, w, ab), lambda b, i: (b, i, 0, 0))
    out_specs = [
        pl.BlockSpec((1, wu.shape[1] // LANES, tm, LANES), lambda b, i: (b, 0, i, 0)),
        spec_t(wdq),
        pl.BlockSpec((1, tm, wdk), lambda b, i: (b, i, 0)),
        spec_t(wdv),
        spec_t(wmq),
        pl.BlockSpec((1, tm, wmk), lambda b, i: (b, i, 0)),
        spec_t(wmv),
    ]
    in_specs = [
        pl.BlockSpec((1, tm, d), lambda b, i: (b, i, 0)),
        pl.BlockSpec((1, 6, d), lambda b, i: (b, 0, 0)),
        _const_spec((1, d)),
        _const_spec(wu.shape),
        _const_spec(wt.shape),
        pl.BlockSpec((1, ROPE_HALF, tm), lambda b, i: (b, 0, i)),
        pl.BlockSpec((1, ROPE_HALF, tm), lambda b, i: (b, 0, i)),
    ]
    return pl.pallas_call(
        kern, grid=(bsz, ns), in_specs=in_specs, out_specs=out_specs, out_shape=out_shape,
        compiler_params=_params("parallel", "parallel"), name="projection",
    )(x, mod, g, wu, wt, cos_t, sin_t)


def _gelu_tanh(y):
    return 0.5 * y * (1.0 + jnp.tanh(math.sqrt(2.0 / math.pi) * (y + 0.044715 * (y * y * y))))


def _ssm_kernel(u_ref, w1_ref, w2_ref, w3_ref, coef_ref, d_ref, gw_ref, gb_ref, ng_ref,
                y_ref, st_ref, yt_ref, carry_ref, *, n_state):
    n = n_state
    nl = SSM_CHUNK

    @pl.when(pl.program_id(1) == 0)
    def _():
        carry_ref[...] = jnp.zeros_like(carry_ref)

    ns, tm = u_ref.shape[1], u_ref.shape[2]
    w = ns * LANES
    mc = tm // nl
    uc = jnp.concatenate([u_ref[0, j, pl.ds(t, mc, stride=nl), :] for t in range(nl) for j in range(ns)],
                         axis=1).astype(BF16)
    st_ref[...] = jnp.dot(uc, w2_ref[...], preferred_element_type=F32)

    def cmul_add(xr, xi, ar, ai, sr, si):
        return xr + (ar * sr - ai * si), xi + (ar * si + ai * sr)

    first_row = lax.broadcasted_iota(jnp.int32, (SUBLANES, n), 0) == 0

    def group(r, carry):
        cr, ci = carry
        rows = pl.ds(pl.multiple_of(r * SUBLANES, SUBLANES), SUBLANES)
        xr = st_ref[rows, 0:n]
        xi = st_ref[rows, n:2 * n]
        for lvl, shift in enumerate((1, 2, 4)):
            sr = pltpu.roll(xr, shift, axis=0)
            si = pltpu.roll(xi, shift, axis=0)
            xr, xi = cmul_add(xr, xi, coef_ref[2 * lvl], coef_ref[2 * lvl + 1], sr, si)
        xr, xi = cmul_add(xr, xi, coef_ref[6], coef_ref[7], cr, ci)
        st_ref[rows, 0:n] = jnp.where(first_row, cr, pltpu.roll(xr, 1, axis=0))
        st_ref[rows, n:2 * n] = jnp.where(first_row, ci, pltpu.roll(xi, 1, axis=0))
        last = SUBLANES - 1
        return (jnp.broadcast_to(xr[last:last + 1], xr.shape),
                jnp.broadcast_to(xi[last:last + 1], xi.shape))

    cr, ci = lax.fori_loop(0, mc // SUBLANES, group, (carry_ref[0], carry_ref[1]))
    carry_ref[0] = cr
    carry_ref[1] = ci

    yc = (jnp.dot(uc, w1_ref[...], preferred_element_type=F32)
          + jnp.dot(st_ref[...].astype(BF16), w3_ref[...], preferred_element_type=F32))
    for t in range(nl):
        for j in range(ns):
            yt_ref[j, pl.ds(t, mc, stride=nl), :] = yc[:, t * w + j * LANES:t * w + (j + 1) * LANES]
    u = jnp.concatenate([u_ref[0, j] for j in range(ns)], axis=1)
    y = jnp.concatenate([yt_ref[j] for j in range(ns)], axis=1) + d_ref[...] * u
    y = _gelu_tanh(y)
    z = jnp.dot(y.astype(BF16), gw_ref[...], preferred_element_type=F32) + gb_ref[...]
    y = y * jax.nn.sigmoid(z)
    y_ref[0] = _rmsnorm_rows(y, ng_ref[...]).astype(y_ref.dtype)


def _ssm(u, w1, w2, w3, coef, d_skip, glu_w, glu_b, norm_g):
    bsz, ns, s, _ = u.shape
    w = ns * LANES
    n = coef.shape[-1]
    tm = SSM_TILE
    kern = functools.partial(_ssm_kernel, n_state=n)
    in_specs = [
        pl.BlockSpec((1, ns, tm, LANES), lambda b, i: (b, 0, i, 0)),
        _const_spec(w1.shape), _const_spec(w2.shape), _const_spec(w3.shape), _const_spec(coef.shape),
        _const_spec((1, w)), _const_spec((w, w)), _const_spec((1, w)), _const_spec((1, w)),
    ]
    return pl.pallas_call(
        kern, grid=(bsz, s // tm), in_specs=in_specs,
        out_specs=pl.BlockSpec((1, tm, w), lambda b, i: (b, i, 0)),
        out_shape=jax.ShapeDtypeStruct((bsz, s, w), BF16),
        scratch_shapes=[pltpu.VMEM((tm // SSM_CHUNK, 2 * n), F32), pltpu.VMEM((ns, tm, LANES), F32),
                        pltpu.VMEM((2, SUBLANES, n), F32)],
        compiler_params=_params("parallel", "arbitrary"), name="ssm",
    )(u, w1, w2, w3, coef, d_skip, glu_w, glu_b, norm_g)


def _row_groups(x):
    rows, lanes = x.shape
    return x.reshape(4, rows // (4 * SUBLANES), SUBLANES, lanes)


def _score_stage(k_ref, first_blk, n_gran, qz, tile_bias, s_ref, mx_ref):
    ab = ATTN_BLOCK
    for w in range(n_gran):
        kb = k_ref[0, pl.ds(pl.multiple_of((first_blk + w) * ab, ab), ab), :]
        s = jnp.dot(kb, qz, preferred_element_type=F32)
        tb = tile_bias(w)
        if tb is not None:
            s = s + tb
        s_ref[w * ab:(w + 1) * ab, :] = s
        mx_ref[w] = jnp.max(jnp.max(_row_groups(s), axis=1), axis=0)


def _softmax_stage(s_ref, mx_ref, vt_ref, first_blk, n_gran, v_rows, row_bias, m, l, acc_ref):
    ab = ATTN_BLOCK
    rbs = [row_bias(w) for w in range(n_gran)]
    raw = mx_ref[0]
    seen = raw if rbs[0] is None else raw + rbs[0]
    for w in range(1, n_gran):
        raw = jnp.maximum(raw, mx_ref[w])
        seen = jnp.maximum(seen, mx_ref[w] if rbs[w] is None else mx_ref[w] + rbs[w])
    m_new = jnp.maximum(m, jnp.max(seen, axis=0, keepdims=True))
    shift = jnp.where(m_new > 0.5 * NEG, m_new, jnp.max(raw, axis=0, keepdims=True))
    alpha = jnp.exp2(m - m_new)
    psum = None
    pv = None
    for w in range(n_gran):
        p = jnp.exp2(s_ref[w * ab:(w + 1) * ab, :] - (shift if rbs[w] is None else shift - rbs[w]))
        part = jnp.sum(jnp.sum(_row_groups(p), axis=1), axis=0)
        psum = part if psum is None else psum + part
        d = jnp.dot(vt_ref[0, first_blk + w, v_rows, :], p.astype(BF16), preferred_element_type=F32)
        pv = d if pv is None else pv + d
    acc_ref[...] = alpha * acc_ref[...] + pv
    return m_new, alpha * l + jnp.sum(psum, axis=0, keepdims=True)


def _flash_tile(g, k_ref, vt_ref, chains, full_bias, diag_bias, qz_ref, s_ref, mx_ref, acc_ref):
    nc = len(chains)
    t = s_ref.shape[-1]
    acc_ref[...] = jnp.zeros(acc_ref.shape, F32)
    neg = jnp.full((1, t), NEG, F32)
    zero = jnp.zeros((1, t), F32)

    def score(c, first, n_gran, bias):
        _score_stage(k_ref, first, n_gran, qz_ref[c], bias[0], s_ref.at[c], mx_ref.at[c])

    def softmax(c, first, n_gran, bias, m, l):
        return _softmax_stage(s_ref.at[c], mx_ref.at[c], vt_ref, first, n_gran, chains[c][1], bias[1],
                              m, l, acc_ref.at[c])

    score(0, 0, ATTN_CHUNK, full_bias(0, 0))

    def step(i, carry):
        first = i * ATTN_CHUNK
        out = []
        for c in range(nc):
            if c + 1 < nc:
                score(c + 1, first, ATTN_CHUNK, full_bias(c + 1, first))
            else:
                score(0, first + ATTN_CHUNK, ATTN_CHUNK, full_bias(0, first + ATTN_CHUNK))
            out += softmax(c, first, ATTN_CHUNK, full_bias(c, first), carry[2 * c], carry[2 * c + 1])
        return tuple(out)

    carry = lax.fori_loop(0, g, step, (neg, zero) * nc)
    first = g * ATTN_CHUNK
    out = []
    score(0, first, chains[0][0] + 1, diag_bias(0))
    for c in range(nc):
        if c + 1 < nc:
            score(c + 1, first, chains[c + 1][0] + 1, diag_bias(c + 1))
        out.append(softmax(c, first, chains[c][0] + 1, diag_bias(c), carry[2 * c], carry[2 * c + 1]))
    return out


def _flash_scratch(nc, dv):
    ab = ATTN_BLOCK
    return [pltpu.VMEM((nc, 2 * HEAD_DIM, ab), BF16),
            pltpu.VMEM((nc, ATTN_CHUNK * ab, ab), F32),
            pltpu.VMEM((nc, ATTN_CHUNK, SUBLANES, ab), F32),
            pltpu.VMEM((nc, dv, ab), F32),
            pltpu.VMEM((ab, ab), F32)]


def _causal_bias(t):
    key = lax.broadcasted_iota(jnp.int32, (t, t), 0)
    qry = lax.broadcasted_iota(jnp.int32, (t, t), 1)
    return jnp.where(key <= qry, 0.0, NEG)


def _diff_kernel(lam_ref, g_ref, qt_ref, k_ref, vt_ref, o_ref, qz_ref, s_ref, mx_ref, acc_ref, tri_ref,
                 *, lam_init):
    g = pl.program_id(2)
    ab = ATTN_BLOCK
    chains = [(qb, slice(None)) for qb in range(ATTN_CHUNK) for _ in range(2)]
    tri_ref[...] = _causal_bias(ab)
    row = lax.broadcasted_iota(jnp.int32, (2 * HEAD_DIM, ab), 0)
    for qb in range(ATTN_CHUNK):
        qt = qt_ref[0, qb].astype(F32)
        qz_ref[2 * qb] = jnp.where(row < HEAD_DIM, qt, 0.0).astype(BF16)
        qz_ref[2 * qb + 1] = jnp.where(row >= HEAD_DIM, qt, 0.0).astype(BF16)

    none = lambda w: None

    def diag_bias(c):
        qb = chains[c][0]
        return (lambda w: tri_ref[...] if w == qb else None), none

    ml = _flash_tile(g, k_ref, vt_ref, chains, lambda c, first: (none, none), diag_bias,
                     qz_ref, s_ref, mx_ref, acc_ref)

    lp = lam_ref[...]
    lam = (jnp.exp(jnp.sum(lp[0:1] * lp[1:2], axis=1, keepdims=True))
           - jnp.exp(jnp.sum(lp[2:3] * lp[3:4], axis=1, keepdims=True)) + lam_init)
    for qb in range(ATTN_CHUNK):
        o = acc_ref[2 * qb] / ml[2 * qb][1] - lam * (acc_ref[2 * qb + 1] / ml[2 * qb + 1][1])
        o = o * lax.rsqrt(jnp.mean(o * o, axis=0, keepdims=True) + EPS) * g_ref[...]
        o_ref[0, qb * ab:(qb + 1) * ab, :] = (o * (1.0 - lam_init)).T.astype(o_ref.dtype)


def _diff_attention(qt, k, vt, lam_params, subln_g, lam_init):
    bsz, nb, wq, ab = qt.shape
    s = k.shape[1]
    dv = 2 * HEAD_DIM
    nh = wq // dv
    tq = ATTN_CHUNK * ab
    kern = functools.partial(_diff_kernel, lam_init=lam_init)
    in_specs = [
        _const_spec(lam_params.shape),
        _const_spec((dv, 1)),
        pl.BlockSpec((1, ATTN_CHUNK, dv, ab), lambda b, h, i: (b, i, h, 0)),
        pl.BlockSpec((1, s, dv), lambda b, h, i: (b, 0, h)),
        pl.BlockSpec((1, nb, dv, ab), lambda b, h, i: (b, 0, h, 0)),
    ]
    return pl.pallas_call(
        kern, grid=(bsz, nh, nb // ATTN_CHUNK), in_specs=in_specs,
        out_specs=pl.BlockSpec((1, tq, dv), lambda b, h, i: (b, i, h)),
        out_shape=jax.ShapeDtypeStruct((bsz, s, wq), BF16),
        scratch_shapes=_flash_scratch(2 * ATTN_CHUNK, dv),
        compiler_params=_params("parallel", "parallel", "arbitrary"), name="diff_attention",
    )(lam_params, subln_g.reshape(dv, 1), qt, k, vt)


def _moba_kernel(g_ref, qt_ref, k_ref, vt_ref, o_ref, kmean_ref, bias_ref, qz_ref, s_ref, mx_ref, acc_ref,
                 tri_ref, *, n_blocks):
    g = pl.program_id(2)
    ab = ATTN_BLOCK
    tq = ATTN_CHUNK * ab
    pair = 2 * HEAD_DIM

    @pl.when(g == 0)
    def _():
        def mean_block(j, c):
            kb = k_ref[0, pl.ds(pl.multiple_of(j * ab, ab), ab), :].astype(F32)
            kmean_ref[pl.ds(j, 1), :] = jnp.mean(kb, axis=0, keepdims=True)
            return c
        lax.fori_loop(0, n_blocks, mean_block, 0)

    tri_ref[...] = _causal_bias(ab)
    chains = [(qb, slice(hh * HEAD_DIM, (hh + 1) * HEAD_DIM)) for qb in range(ATTN_CHUNK) for hh in range(2)]
    qt = jnp.concatenate([qt_ref[0, qb] for qb in range(ATTN_CHUNK)], axis=1).astype(F32)
    row = lax.broadcasted_iota(jnp.int32, (pair, tq), 0)
    lane = lax.broadcasted_iota(jnp.int32, (n_blocks, pair), 1)
    bid = lax.broadcasted_iota(jnp.int32, (n_blocks, tq), 0)
    qblk = g * ATTN_CHUNK + lax.broadcasted_iota(jnp.int32, (n_blocks, tq), 1) // ab
    past = bid < qblk
    bid = bid.astype(F32)
    kmean = kmean_ref[...]
    for hh in range(2):
        qf = jnp.where((row >= hh * HEAD_DIM) & (row < (hh + 1) * HEAD_DIM), qt, 0.0)
        for qb in range(ATTN_CHUNK):
            qz_ref[2 * qb + hh] = qf[:, qb * ab:(qb + 1) * ab].astype(BF16)
        km = jnp.where((lane >= hh * HEAD_DIM) & (lane < (hh + 1) * HEAD_DIM), kmean, 0.0)
        gate = jnp.dot(km, qf, preferred_element_type=F32, precision=lax.Precision.HIGHEST)
        gate = jnp.where(past, gate, NEG)
        sel = jnp.zeros_like(gate)
        for _ in range(MOBA_TOPK):
            top = jnp.max(gate, axis=0, keepdims=True)
            first = jnp.min(jnp.where(gate == top, bid, float(n_blocks)), axis=0, keepdims=True)
            hit = bid == first
            sel = jnp.where(hit, 1.0, sel)
            gate = jnp.where(hit, BELOW_NEG, gate)
        bias_ref[hh] = jnp.where(past, jnp.where(sel > 0.0, 0.0, NEG), NEG)

    none = lambda w: None

    def picked(c, blk):
        qb, hh = c // 2, c % 2
        return bias_ref[hh, pl.ds(blk, 1), qb * ab:(qb + 1) * ab]

    def full_bias(c, first):
        return none, lambda w: picked(c, first + w)

    def diag_bias(c):
        qb = chains[c][0]
        return ((lambda w: tri_ref[...] if w == qb else None),
                (lambda w: None if w == qb else picked(c, g * ATTN_CHUNK + w)))

    ml = _flash_tile(g, k_ref, vt_ref, chains, full_bias, diag_bias, qz_ref, s_ref, mx_ref, acc_ref)

    for qb in range(ATTN_CHUNK):
        outs = []
        for hh in range(2):
            o = acc_ref[2 * qb + hh] / ml[2 * qb + hh][1]
            outs.append(o * lax.rsqrt(jnp.mean(o * o, axis=0, keepdims=True) + EPS) * g_ref[...])
        o_ref[0, qb * ab:(qb + 1) * ab, :] = jnp.concatenate(outs, axis=0).T.astype(o_ref.dtype)


def _moba_attention(qt, k, vt, norm_g):
    bsz, nb, wq, ab = qt.shape
    s = k.shape[1]
    pair = 2 * HEAD_DIM
    tq = ATTN_CHUNK * ab
    kern = functools.partial(_moba_kernel, n_blocks=nb)
    in_specs = [
        _const_spec((HEAD_DIM, 1)),
        pl.BlockSpec((1, ATTN_CHUNK, pair, ab), lambda b, h, i: (b, i, h, 0)),
        pl.BlockSpec((1, s, pair), lambda b, h, i: (b, 0, h)),
        pl.BlockSpec((1, nb, pair, ab), lambda b, h, i: (b, 0, h, 0)),
    ]
    return pl.pallas_call(
        kern, grid=(bsz, wq // pair, nb // ATTN_CHUNK), in_specs=in_specs,
        out_specs=pl.BlockSpec((1, tq, pair), lambda b, h, i: (b, i, h)),
        out_shape=jax.ShapeDtypeStruct((bsz, s, wq), BF16),
        scratch_shapes=[pltpu.VMEM((nb, pair), F32), pltpu.VMEM((2, nb, tq), F32)]
        + _flash_scratch(2 * ATTN_CHUNK, HEAD_DIM),
        compiler_params=_params("parallel", "parallel", "arbitrary"), name="moba_attention",
    )(norm_g.reshape(HEAD_DIM, 1), qt, k, vt)


def _out_mlp_kernel(x_ref, ys_ref, yd_ref, ym_ref, mod_ref, g2_ref, wo_ref, w1_ref, w2_ref, fg_ref,
                    o_ref, *, final_norm, ff_chunk):
    x = x_ref[0]
    mod = mod_ref[0]
    ws = ys_ref.shape[2]
    wd = yd_ref.shape[2]
    mix = (jnp.dot(ys_ref[0], wo_ref[0:ws], preferred_element_type=F32)
           + jnp.dot(yd_ref[0], wo_ref[ws:ws + wd], preferred_element_type=F32)
           + jnp.dot(ym_ref[0], wo_ref[ws + wd:], preferred_element_type=F32))
    x = x + mod[2:3] * mix
    h = (_rmsnorm_rows(x, g2_ref[...]) * (1.0 + mod[4:5]) + mod[3:4]).astype(BF16)
    acc = jnp.zeros_like(x)
    for c in range(w1_ref.shape[1] // ff_chunk):
        a = jnp.dot(h, w1_ref[:, c * ff_chunk:(c + 1) * ff_chunk], preferred_element_type=F32)
        a = jnp.square(jnp.maximum(a, 0.0)).astype(BF16)
        acc = acc + jnp.dot(a, w2_ref[c * ff_chunk:(c + 1) * ff_chunk, :], preferred_element_type=F32)
    x = x + mod[5:6] * acc
    if final_norm:
        x = _rmsnorm_rows(x, fg_ref[...])
    o_ref[0] = x


def _out_mlp(x, ys, yd, ym, mod, g2, wo, w1, w2, final_g, final_norm):
    bsz, s, d = x.shape
    tm = TOKEN_TILE
    kern = functools.partial(_out_mlp_kernel, final_norm=final_norm, ff_chunk=d)
    tok = lambda w: pl.BlockSpec((1, tm, w), lambda b, i: (b, i, 0))
    in_specs = [
        tok(d), tok(ys.shape[2]), tok(yd.shape[2]), tok(ym.shape[2]),
        pl.BlockSpec((1, 6, d), lambda b, i: (b, 0, 0)),
        _const_spec((1, d)), _const_spec(wo.shape), _const_spec(w1.shape), _const_spec(w2.shape),
        _const_spec((1, d)),
    ]
    return pl.pallas_call(
        kern, grid=(bsz, s // tm), in_specs=in_specs, out_specs=tok(d),
        out_shape=jax.ShapeDtypeStruct((bsz, s, d), F32),
        compiler_params=_params("parallel", "parallel"), name="out_mlp",
    )(x, ys, yd, ym, mod, g2, wo, w1, w2, final_g)


def kernel(x, c, positions, norm1_g, norm2_g, w_ada, b_ada, w_in, w_out, ssm_a_re, ssm_a_im, ssm_log_dt, ssm_b_re, ssm_b_im, ssm_c_re, ssm_c_im, ssm_d, ssm_glu_w, ssm_glu_b, ssm_norm_g, diff_lq1, diff_lk1, diff_lq2, diff_lk2, diff_subln_g, moba_norm_g, mlp_w1, mlp_w2, final_g):
    bsz, s, d = x.shape
    depth = w_in.shape[0]
    w_ssm = ssm_d.shape[1]
    w_diff = DIFF_HEADS * 2 * HEAD_DIM
    w_moba = MOBA_HEADS * HEAD_DIM
    assert s % TOKEN_TILE == 0 and TOKEN_TILE % ATTN_BLOCK == 0 and s % (ATTN_CHUNK * ATTN_BLOCK) == 0
    assert s % SSM_TILE == 0 and SSM_TILE % (SSM_CHUNK * SUBLANES) == 0
    assert w_in.shape[2] == w_ssm + 3 * w_diff + 3 * w_moba

    cos_t, sin_t = _rope_tables(positions)
    mod_all = _ada_mod(c, w_ada, b_ada).reshape(depth, bsz, 6, d)

    o_dq = w_ssm
    o_dk = o_dq + w_diff
    o_dv = o_dk + w_diff
    o_mq = o_dv + w_diff
    o_mk = o_mq + w_moba
    o_mv = o_mk + w_moba
    widths = (w_diff, w_diff, w_moba, w_moba, w_diff, w_moba)

    for l in range(depth):
        wl = w_in[l]
        wu = wl[:, :o_dq].astype(BF16)
        wt = jnp.concatenate([wl[:, o_dq:o_dk], wl[:, o_dk:o_dv], wl[:, o_mq:o_mk], wl[:, o_mk:o_mv],
                              wl[:, o_dv:o_mq], wl[:, o_mv:]], axis=1).T.astype(BF16)
        mod = mod_all[l]
        u, dq, dk, dv, mq, mk, mv = _projection(x, mod, norm1_g[l][None], wu, wt, cos_t, sin_t, widths)

        w1, w2, w3, coef = _ssm_prep(ssm_a_re[l], ssm_a_im[l], ssm_log_dt[l], ssm_b_re[l], ssm_b_im[l],
                                     ssm_c_re[l], ssm_c_im[l])
        y_ssm = _ssm(u, w1, w2, w3, coef, ssm_d[l][None], ssm_glu_w[l].astype(BF16),
                     ssm_glu_b[l][None], ssm_norm_g[l][None])

        lam_init = 0.8 - 0.6 * math.exp(-0.3 * l)
        lam_params = jnp.stack([diff_lq1[l], diff_lk1[l], diff_lq2[l], diff_lk2[l]]).astype(F32)
        y_diff = _diff_attention(dq, dk, dv, lam_params, diff_subln_g[l], lam_init)
        y_moba = _moba_attention(mq, mk, mv, moba_norm_g[l])

        x = _out_mlp(x, y_ssm, y_diff, y_moba, mod, norm2_g[l][None], w_out[l].astype(BF16),
                     mlp_w1[l].astype(BF16), mlp_w2[l].astype(BF16), final_g[None],
                     final_norm=(l == depth - 1))
    return x
```

```python
import functools
import math

import jax
import jax.numpy as jnp
from jax import lax
from jax.experimental import pallas as pl
from jax.experimental.pallas import tpu as pltpu

F32 = jnp.float32
BF16 = jnp.bfloat16

SSM_GROUP = 16
SSM_STATE = 64
HEAD_DIM = 64
DIFF_HEADS = 4
MOBA_HEADS = 4
MOBA_BLOCK = 256
MOBA_TOPK = 3
ROPE_THETA = 500000.0
ROPE_HALF = 8
EPS = 1e-6
LOG2_E = math.log2(math.e)
NEG = -1e30
BELOW_NEG = -3e38

SUBLANES = 8
LANES = 128
TOKEN_TILE = 512
SSM_CHUNK = 4
SSM_TILE = 2048
ATTN_BLOCK = MOBA_BLOCK
ATTN_CHUNK = 4
VMEM_LIMIT = 56 * 1024 * 1024


def _params(*sem):
    return pltpu.CompilerParams(dimension_semantics=sem, vmem_limit_bytes=VMEM_LIMIT)


def _const_spec(shape):
    nd = len(shape)
    return pl.BlockSpec(shape, lambda *_: (0,) * nd)


def _rope_kernel(inv_ref, pos_ref, cos_ref, sin_ref):
    ang = inv_ref[...] * pos_ref[0].astype(F32)
    cos_ref[0] = jnp.cos(ang)
    sin_ref[0] = jnp.sin(ang)


def _rope_tables(positions):
    bsz, s = positions.shape
    rot = 2 * ROPE_HALF
    inv = (ROPE_THETA ** (-jnp.arange(0, rot, 2, dtype=F32) / rot)).reshape(ROPE_HALF, 1)
    out = jax.ShapeDtypeStruct((bsz, ROPE_HALF, s), F32)
    return pl.pallas_call(
        _rope_kernel,
        grid=(bsz,),
        in_specs=[_const_spec((ROPE_HALF, 1)),
                  pl.BlockSpec((1, 1, s), lambda b: (b, 0, 0))],
        out_specs=[pl.BlockSpec((1, ROPE_HALF, s), lambda b: (b, 0, 0))] * 2,
        out_shape=[out, out],
        compiler_params=_params("parallel"),
        name="rope_tables",
    )(inv, positions.reshape(bsz, 1, s))


def _ada_kernel(c_ref, w_ref, b_ref, o_ref):
    c = c_ref[...]
    sc = c * jax.nn.sigmoid(c)
    o_ref[0] = jnp.dot(sc, w_ref[0], preferred_element_type=F32,
                       precision=lax.Precision.HIGHEST) + b_ref[0]


def _ada_mod(c, w_ada, b_ada):
    depth, d, d6 = w_ada.shape
    bsz = c.shape[0]
    nchunk = d6 // d
    return pl.pallas_call(
        _ada_kernel,
        grid=(depth, nchunk),
        in_specs=[_const_spec((bsz, d)),
                  pl.BlockSpec((1, d, d), lambda l, j: (l, 0, j)),
                  pl.BlockSpec((1, 1, d), lambda l, j: (l, 0, j))],
        out_specs=pl.BlockSpec((1, bsz, d), lambda l, j: (l, 0, j)),
        out_shape=jax.ShapeDtypeStruct((depth, bsz, d6), F32),
        compiler_params=_params("parallel", "parallel"),
        name="ada_mod",
    )(c, w_ada, b_ada.reshape(depth, 1, d6))


def _ssm_prep_kernel(are_ref, aim_ref, ldt_ref, bre_ref, bim_ref,
                     bbr_ref, bbi_ref, p1r_ref, p1i_ref, pcr_ref, pci_ref):
    ar = are_ref[...]
    ai = aim_ref[...]
    dt = jnp.exp(ldt_ref[...])
    mag = jnp.exp(ar * dt)
    abr = mag * jnp.cos(ai * dt)
    abi = mag * jnp.sin(ai * dt)
    den = ar * ar + ai * ai
    nr = abr - 1.0
    ni = abi
    kr = (nr * ar + ni * ai) / den
    ki = (ni * ar - nr * ai) / den
    br = bre_ref[...]
    bi = bim_ref[...]
    bbr_ref[...] = kr[None] * br - ki[None] * bi
    bbi_ref[...] = kr[None] * bi + ki[None] * br
    pr, pi = abr, abi
    p1r_ref[0] = pr
    p1i_ref[0] = pi
    for j in range(1, SSM_CHUNK):
        pr, pi = pr * abr - pi * abi, pr * abi + pi * abr
        p1r_ref[j] = pr
        p1i_ref[j] = pi
    cr, ci = pr, pi
    pcr_ref[0] = cr
    pci_ref[0] = ci
    for j in range(1, SUBLANES):
        pr, pi = pr * cr - pi * ci, pr * ci + pi * cr
        pcr_ref[j] = pr
        pci_ref[j] = pi


def _ssm_prep(a_re, a_im, log_dt, b_re, b_im, c_re, c_im):
    depth, g, p = a_re.shape
    cch = b_re.shape[-1]
    gg = depth * g
    bt = lambda b: jnp.transpose(b, (3, 0, 1, 2)).reshape(cch, gg, p)
    shapes = ([jax.ShapeDtypeStruct((cch, gg, p), F32)] * 2 + [jax.ShapeDtypeStruct((SSM_CHUNK, gg, p), F32)] * 2
              + [jax.ShapeDtypeStruct((SUBLANES, gg, p), F32)] * 2)
    outs = pl.pallas_call(
        _ssm_prep_kernel, out_shape=shapes, name="ssm_prep",
    )(a_re.reshape(gg, p), a_im.reshape(gg, p), log_dt.reshape(gg, 1), bt(b_re), bt(b_im))
    per_layer = [x.reshape(x.shape[0], depth, g, p) for x in outs]
    return jax.vmap(_ssm_matrices, in_axes=(1, 1, 1, 1, 1, 1, 0, 0))(*per_layer, c_re, c_im)


def _ssm_matrices(bbr, bbi, p1r, p1i, pcr, pci, c_re, c_im):
    cch, g, p = bbr.shape
    w = g * cch
    n = g * p
    nl = SSM_CHUNK
    hi = lax.Precision.HIGHEST
    eye = jnp.eye(g, dtype=F32)
    ar = jnp.concatenate([jnp.ones((1, g, p), F32), p1r], axis=0)
    ai = jnp.concatenate([jnp.zeros((1, g, p), F32), p1i], axis=0)
    car = c_re[None] * ar[:, :, None, :] - c_im[None] * ai[:, :, None, :]
    cai = c_re[None] * ai[:, :, None, :] + c_im[None] * ar[:, :, None, :]
    kk = (jnp.einsum('tgcp,dgp->tgcd', car[:nl], bbr, precision=hi)
          - jnp.einsum('tgcp,dgp->tgcd', cai[:nl], bbi, precision=hi))
    zk = jnp.zeros_like(kk[0])
    ksel = jnp.stack([jnp.stack([kk[t2 - t] if t2 >= t else zk for t2 in range(nl)])
                      for t in range(nl)])
    w1 = jnp.einsum('stgcd,gh->sgdthc', ksel, eye).reshape(nl * w, nl * w)
    rev_r = jnp.stack([ar[nl - 1 - t] for t in range(nl)])
    rev_i = jnp.stack([ai[nl - 1 - t] for t in range(nl)])
    abr_ = rev_r[:, None] * bbr[None] - rev_i[:, None] * bbi[None]
    abi_ = rev_r[:, None] * bbi[None] + rev_i[:, None] * bbr[None]
    blk2 = lambda x: jnp.einsum('tdgp,gh->tgdhp', x, eye).reshape(nl * w, n)
    w2 = jnp.concatenate([blk2(abr_), blk2(abi_)], axis=1)
    blk3 = lambda x: jnp.einsum('tgcp,gh->gpthc', x, eye).reshape(n, nl * w)
    w3 = jnp.concatenate([blk3(car[1:]), -blk3(cai[1:])], axis=0)
    pcr = pcr.reshape(SUBLANES, n)
    pci = pci.reshape(SUBLANES, n)
    row = jnp.arange(SUBLANES)[:, None]
    tabs = []
    for shift in (1, 2, 4):
        keep = row >= shift
        tabs.append(jnp.where(keep, pcr[shift - 1][None], 0.0))
        tabs.append(jnp.where(keep, pci[shift - 1][None], 0.0))
    tabs += [pcr, pci]
    return w1.astype(BF16), w2.astype(BF16), w3.astype(BF16), jnp.stack(tabs)


def _rmsnorm_rows(x, g):
    return x * lax.rsqrt(jnp.mean(x * x, axis=-1, keepdims=True) + EPS) * g


def _proj_kernel(x_ref, mod_ref, g_ref, wu_ref, wt_ref, cos_ref, sin_ref,
                 u_ref, dq_ref, dk_ref, dv_ref, mq_ref, mk_ref, mv_ref, *, n_rot, widths):
    wdq, wdk, wmq, wmk, wdv, wmv = widths
    x = x_ref[0]
    mod = mod_ref[0]
    h = _rmsnorm_rows(x, g_ref[...]) * (1.0 + mod[1:2]) + mod[0:1]
    hb = h.astype(BF16)
    u = jnp.dot(hb, wu_ref[...], preferred_element_type=F32)
    for j in range(u_ref.shape[1]):
        u_ref[0, j] = u[:, j * LANES:(j + 1) * LANES]
    pt = lax.dot_general(wt_ref[...], hb, (((1,), (1,)), ((), ())), preferred_element_type=F32)
    tm = pt.shape[1]
    rot = pt[:n_rot].reshape(n_rot // HEAD_DIM, HEAD_DIM, tm)
    cos = cos_ref[0][None]
    sin = sin_ref[0][None]
    x1 = rot[:, 0:ROPE_HALF]
    x2 = rot[:, ROPE_HALF:2 * ROPE_HALF]
    rot = jnp.concatenate([x1 * cos - x2 * sin, x2 * cos + x1 * sin, rot[:, 2 * ROPE_HALF:]], axis=1)
    rot = rot.reshape(n_rot, tm)
    qscale = HEAD_DIM ** -0.5 * LOG2_E
    o = 0
    dq = (rot[o:o + wdq] * qscale).astype(BF16); o += wdq
    dk_ref[0] = rot[o:o + wdk].T.astype(BF16); o += wdk
    mq = (rot[o:o + wmq] * qscale).astype(BF16); o += wmq
    mk_ref[0] = rot[o:o + wmk].T.astype(BF16); o += wmk
    dv = pt[o:o + wdv].astype(BF16); o += wdv
    mv = pt[o:o + wmv].astype(BF16)
    for j in range(tm // ATTN_BLOCK):
        blk = slice(j * ATTN_BLOCK, (j + 1) * ATTN_BLOCK)
        dq_ref[0, j] = dq[:, blk]
        dv_ref[0, j] = dv[:, blk]
        mq_ref[0, j] = mq[:, blk]
        mv_ref[0, j] = mv[:, blk]


def _projection(x, mod, g, wu, wt, cos_t, sin_t, widths):
    bsz, s, d = x.shape
    tm = TOKEN_TILE
    ns = s // tm
    wdq, wdk, wmq, wmk, wdv, wmv = widths
    n_rot = wdq + wdk + wmq + wmk
    ab = ATTN_BLOCK
    nab = tm // ab
    kern = functools.partial(_proj_kernel, n_rot=n_rot, widths=widths)
    tiles_t = lambda w: jax.ShapeDtypeStruct((bsz, s // ab, w, ab), BF16)
    out_shape = [
        jax.ShapeDtypeStruct((bsz, wu.shape[1] // LANES, s, LANES), F32),
        tiles_t(wdq),
        jax.ShapeDtypeStruct((bsz, s, wdk), BF16),
        tiles_t(wdv),
        tiles_t(wmq),
        jax.ShapeDtypeStruct((bsz, s, wmk), BF16),
        tiles_t(wmv),
    ]
    spec_t = lambda w: pl.BlockSpec((1, nab, w, ab), lambda b, i: (b, i, 0, 0))
    out_specs = [
        pl.BlockSpec((1, wu.shape[1] // LANES, tm, LANES), lambda b, i: (b, 0, i, 0)),
        spec_t(wdq),
        pl.BlockSpec((1, tm, wdk), lambda b, i: (b, i, 0)),
        spec_t(wdv),
        spec_t(wmq),
        pl.BlockSpec((1, tm, wmk), lambda b, i: (b, i, 0)),
        spec_t(wmv),
    ]
    in_specs = [
        pl.BlockSpec((1, tm, d), lambda b, i: (b, i, 0)),
        pl.BlockSpec((1, 6, d), lambda b, i: (b, 0, 0)),
        _const_spec((1, d)),
        _const_spec(wu.shape),
        _const_spec(wt.shape),
        pl.BlockSpec((1, ROPE_HALF, tm), lambda b, i: (b, 0, i)),
        pl.BlockSpec((1, ROPE_HALF, tm), lambda b, i: (b, 0, i)),
    ]
    return pl.pallas_call(
        kern, grid=(bsz, ns), in_specs=in_specs, out_specs=out_specs, out_shape=out_shape,
        compiler_params=_params("parallel", "parallel"), name="projection",
    )(x, mod, g, wu, wt, cos_t, sin_t)


def _gelu_tanh(y):
    return 0.5 * y * (1.0 + jnp.tanh(math.sqrt(2.0 / math.pi) * (y + 0.044715 * (y * y * y))))


def _ssm_kernel(u_ref, w1_ref, w2_ref, w3_ref, coef_ref, d_ref, gw_ref, gb_ref, ng_ref,
                y_ref, st_ref, yt_ref, carry_ref, *, n_state):
    n = n_state
    nl = SSM_CHUNK

    @pl.when(pl.program_id(1) == 0)
    def _():
        carry_ref[...] = jnp.zeros_like(carry_ref)

    ns, tm = u_ref.shape[1], u_ref.shape[2]
    w = ns * LANES
    mc = tm // nl
    uc = jnp.concatenate([u_ref[0, j, pl.ds(t, mc, stride=nl), :] for t in range(nl) for j in range(ns)],
                         axis=1).astype(BF16)
    st_ref[...] = jnp.dot(uc, w2_ref[...], preferred_element_type=F32)

    def cmul_add(xr, xi, ar, ai, sr, si):
        return xr + (ar * sr - ai * si), xi + (ar * si + ai * sr)

    first_row = lax.broadcasted_iota(jnp.int32, (SUBLANES, n), 0) == 0

    def group(r, carry):
        cr, ci = carry
        rows = pl.ds(pl.multiple_of(r * SUBLANES, SUBLANES), SUBLANES)
        xr = st_ref[rows, 0:n]
        xi = st_ref[rows, n:2 * n]
        for lvl, shift in enumerate((1, 2, 4)):
            sr = pltpu.roll(xr, shift, axis=0)
            si = pltpu.roll(xi, shift, axis=0)
            xr, xi = cmul_add(xr, xi, coef_ref[2 * lvl], coef_ref[2 * lvl + 1], sr, si)
        xr, xi = cmul_add(xr, xi, coef_ref[6], coef_ref[7], cr, ci)
        st_ref[rows, 0:n] = jnp.where(first_row, cr, pltpu.roll(xr, 1, axis=0))
        st_ref[rows, n:2 * n] = jnp.where(first_row, ci, pltpu.roll(xi, 1, axis=0))
        last = SUBLANES - 1
        return (jnp.broadcast_to(xr[last:last + 1], xr.shape),
                jnp.broadcast_to(xi[last:last + 1], xi.shape))

    cr, ci = lax.fori_loop(0, mc // SUBLANES, group, (carry_ref[0], carry_ref[1]))
    carry_ref[0] = cr
    carry_ref[1] = ci

    yc = (jnp.dot(uc, w1_ref[...], preferred_element_type=F32)
          + jnp.dot(st_ref[...].astype(BF16), w3_ref[...], preferred_element_type=F32))
    for t in range(nl):
        for j in range(ns):
            yt_ref[j, pl.ds(t, mc, stride=nl), :] = yc[:, t * w + j * LANES:t * w + (j + 1) * LANES]
    u = jnp.concatenate([u_ref[0, j] for j in range(ns)], axis=1)
    y = jnp.concatenate([yt_ref[j] for j in range(ns)], axis=1) + d_ref[...] * u
    y = _gelu_tanh(y)
    z = jnp.dot(y.astype(BF16), gw_ref[...], preferred_element_type=F32) + gb_ref[...]
    y = y * jax.nn.sigmoid(z)
    y_ref[0] = _rmsnorm_rows(y, ng_ref[...]).astype(y_ref.dtype)


def _ssm(u, w1, w2, w3, coef, d_skip, glu_w, glu_b, norm_g):
    bsz, ns, s, _ = u.shape
    w = ns * LANES
    n = coef.shape[-1]
    tm = SSM_TILE
    kern = functools.partial(_ssm_kernel, n_state=n)
    in_specs = [
        pl.BlockSpec((1, ns, tm, LANES), lambda b, i: (b, 0, i, 0)),
        _const_spec(w1.shape), _const_spec(w2.shape), _const_spec(w3.shape), _const_spec(coef.shape),
        _const_spec((1, w)), _const_spec((w, w)), _const_spec((1, w)), _const_spec((1, w)),
    ]
    return pl.pallas_call(
        kern, grid=(bsz, s // tm), in_specs=in_specs,
        out_specs=pl.BlockSpec((1, tm, w), lambda b, i: (b, i, 0)),
        out_shape=jax.ShapeDtypeStruct((bsz, s, w), BF16),
        scratch_shapes=[pltpu.VMEM((tm // SSM_CHUNK, 2 * n), F32), pltpu.VMEM((ns, tm, LANES), F32),
                        pltpu.VMEM((2, SUBLANES, n), F32)],
        compiler_params=_params("parallel", "arbitrary"), name="ssm",
    )(u, w1, w2, w3, coef, d_skip, glu_w, glu_b, norm_g)


def _row_groups(x):
    rows, lanes = x.shape
    return x.reshape(4, rows // (4 * SUBLANES), SUBLANES, lanes)


def _score_stage(k_ref, first_blk, n_gran, qz, tile_bias, s_ref, mx_ref):
    ab = ATTN_BLOCK
    for w in range(n_gran):
        kb = k_ref[0, pl.ds(pl.multiple_of((first_blk + w) * ab, ab), ab), :]
        s = jnp.dot(kb, qz, preferred_element_type=F32)
        tb = tile_bias(w)
        if tb is not None:
            s = s + tb
        s_ref[w * ab:(w + 1) * ab, :] = s
        mx_ref[w] = jnp.max(jnp.max(_row_groups(s), axis=1), axis=0)


def _softmax_stage(s_ref, mx_ref, vt_ref, first_blk, n_gran, v_rows, row_bias, m, l, acc_ref):
    ab = ATTN_BLOCK
    rbs = [row_bias(w) for w in range(n_gran)]
    raw = mx_ref[0]
    seen = raw if rbs[0] is None else raw + rbs[0]
    for w in range(1, n_gran):
        raw = jnp.maximum(raw, mx_ref[w])
        seen = jnp.maximum(seen, mx_ref[w] if rbs[w] is None else mx_ref[w] + rbs[w])
    m_new = jnp.maximum(m, jnp.max(seen, axis=0, keepdims=True))
    shift = jnp.where(m_new > 0.5 * NEG, m_new, jnp.max(raw, axis=0, keepdims=True))
    alpha = jnp.exp2(m - m_new)
    psum = None
    pv = None
    for w in range(n_gran):
        p = jnp.exp2(s_ref[w * ab:(w + 1) * ab, :] - (shift if rbs[w] is None else shift - rbs[w]))
        part = jnp.sum(jnp.sum(_row_groups(p), axis=1), axis=0)
        psum = part if psum is None else psum + part
        d = jnp.dot(vt_ref[0, first_blk + w, v_rows, :], p.astype(BF16), preferred_element_type=F32)
        pv = d if pv is None else pv + d
    acc_ref[...] = alpha * acc_ref[...] + pv
    return m_new, alpha * l + jnp.sum(psum, axis=0, keepdims=True)


def _flash_tile(g, k_ref, vt_ref, chains, full_bias, diag_bias, qz_ref, s_ref, mx_ref, acc_ref):
    nc = len(chains)
    t = s_ref.shape[-1]
    acc_ref[...] = jnp.zeros(acc_ref.shape, F32)
    neg = jnp.full((1, t), NEG, F32)
    zero = jnp.zeros((1, t), F32)

    def score(c, first, n_gran, bias):
        _score_stage(k_ref, first, n_gran, qz_ref[c], bias[0], s_ref.at[c], mx_ref.at[c])

    def softmax(c, first, n_gran, bias, m, l):
        return _softmax_stage(s_ref.at[c], mx_ref.at[c], vt_ref, first, n_gran, chains[c][1], bias[1],
                              m, l, acc_ref.at[c])

    score(0, 0, ATTN_CHUNK, full_bias(0, 0))

    def step(i, carry):
        first = i * ATTN_CHUNK
        out = []
        for c in range(nc):
            if c + 1 < nc:
                score(c + 1, first, ATTN_CHUNK, full_bias(c + 1, first))
            else:
                score(0, first + ATTN_CHUNK, ATTN_CHUNK, full_bias(0, first + ATTN_CHUNK))
            out += softmax(c, first, ATTN_CHUNK, full_bias(c, first), carry[2 * c], carry[2 * c + 1])
        return tuple(out)

    carry = lax.fori_loop(0, g, step, (neg, zero) * nc)
    first = g * ATTN_CHUNK
    out = []
    score(0, first, chains[0][0] + 1, diag_bias(0))
    for c in range(nc):
        if c + 1 < nc:
            score(c + 1, first, chains[c + 1][0] + 1, diag_bias(c + 1))
        out.append(softmax(c, first, chains[c][0] + 1, diag_bias(c), carry[2 * c], carry[2 * c + 1]))
    return out


def _flash_scratch(nc, dv):
    ab = ATTN_BLOCK
    return [pltpu.VMEM((nc, 2 * HEAD_DIM, ab), BF16),
            pltpu.VMEM((nc, ATTN_CHUNK * ab, ab), F32),
            pltpu.VMEM((nc, ATTN_CHUNK, SUBLANES, ab), F32),
            pltpu.VMEM((nc, dv, ab), F32),
            pltpu.VMEM((ab, ab), F32)]


def _causal_bias(t):
    key = lax.broadcasted_iota(jnp.int32, (t, t), 0)
    qry = lax.broadcasted_iota(jnp.int32, (t, t), 1)
    return jnp.where(key <= qry, 0.0, NEG)


def _diff_kernel(lam_ref, g_ref, qt_ref, k_ref, vt_ref, o_ref, qz_ref, s_ref, mx_ref, acc_ref, tri_ref,
                 *, lam_init):
    g = pl.program_id(2)
    ab = ATTN_BLOCK
    chains = [(qb, slice(None)) for qb in range(ATTN_CHUNK) for _ in range(2)]
    tri_ref[...] = _causal_bias(ab)
    row = lax.broadcasted_iota(jnp.int32, (2 * HEAD_DIM, ab), 0)
    for qb in range(ATTN_CHUNK):
        qt = qt_ref[0, qb].astype(F32)
        qz_ref[2 * qb] = jnp.where(row < HEAD_DIM, qt, 0.0).astype(BF16)
        qz_ref[2 * qb + 1] = jnp.where(row >= HEAD_DIM, qt, 0.0).astype(BF16)

    none = lambda w: None

    def diag_bias(c):
        qb = chains[c][0]
        return (lambda w: tri_ref[...] if w == qb else None), none

    ml = _flash_tile(g, k_ref, vt_ref, chains, lambda c, first: (none, none), diag_bias,
                     qz_ref, s_ref, mx_ref, acc_ref)

    lp = lam_ref[...]
    lam = (jnp.exp(jnp.sum(lp[0:1] * lp[1:2], axis=1, keepdims=True))
           - jnp.exp(jnp.sum(lp[2:3] * lp[3:4], axis=1, keepdims=True)) + lam_init)
    for qb in range(ATTN_CHUNK):
        o = acc_ref[2 * qb] / ml[2 * qb][1] - lam * (acc_ref[2 * qb + 1] / ml[2 * qb + 1][1])
        o = o * lax.rsqrt(jnp.mean(o * o, axis=0, keepdims=True) + EPS) * g_ref[...]
        o_ref[0, qb * ab:(qb + 1) * ab, :] = (o * (1.0 - lam_init)).T.astype(o_ref.dtype)


def _diff_attention(qt, k, vt, lam_params, subln_g, lam_init):
    bsz, nb, wq, ab = qt.shape
    s = k.shape[1]
    dv = 2 * HEAD_DIM
    nh = wq // dv
    tq = ATTN_CHUNK * ab
    kern = functools.partial(_diff_kernel, lam_init=lam_init)
    in_specs = [
        _const_spec(lam_params.shape),
        _const_spec((dv, 1)),
        pl.BlockSpec((1, ATTN_CHUNK, dv, ab), lambda b, h, i: (b, i, h, 0)),
        pl.BlockSpec((1, s, dv), lambda b, h, i: (b, 0, h)),
        pl.BlockSpec((1, nb, dv, ab), lambda b, h, i: (b, 0, h, 0)),
    ]
    return pl.pallas_call(
        kern, grid=(bsz, nh, nb // ATTN_CHUNK), in_specs=in_specs,
        out_specs=pl.BlockSpec((1, tq, dv), lambda b, h, i: (b, i, h)),
        out_shape=jax.ShapeDtypeStruct((bsz, s, wq), BF16),
        scratch_shapes=_flash_scratch(2 * ATTN_CHUNK, dv),
        compiler_params=_params("parallel", "parallel", "arbitrary"), name="diff_attention",
    )(lam_params, subln_g.reshape(dv, 1), qt, k, vt)


def _moba_kernel(g_ref, qt_ref, k_ref, vt_ref, o_ref, kmean_ref, bias_ref, qz_ref, s_ref, mx_ref, acc_ref,
                 tri_ref, *, n_blocks):
    g = pl.program_id(2)
    ab = ATTN_BLOCK
    tq = ATTN_CHUNK * ab
    pair = 2 * HEAD_DIM

    @pl.when(g == 0)
    def _():
        def mean_block(j, c):
            kb = k_ref[0, pl.ds(pl.multiple_of(j * ab, ab), ab), :].astype(F32)
            kmean_ref[pl.ds(j, 1), :] = jnp.mean(kb, axis=0, keepdims=True)
            return c
        lax.fori_loop(0, n_blocks, mean_block, 0)

    tri_ref[...] = _causal_bias(ab)
    chains = [(qb, slice(hh * HEAD_DIM, (hh + 1) * HEAD_DIM)) for qb in range(ATTN_CHUNK) for hh in range(2)]
    qt = jnp.concatenate([qt_ref[0, qb] for qb in range(ATTN_CHUNK)], axis=1).astype(F32)
    row = lax.broadcasted_iota(jnp.int32, (pair, tq), 0)
    lane = lax.broadcasted_iota(jnp.int32, (n_blocks, pair), 1)
    bid = lax.broadcasted_iota(jnp.int32, (n_blocks, tq), 0)
    qblk = g * ATTN_CHUNK + lax.broadcasted_iota(jnp.int32, (n_blocks, tq), 1) // ab
    past = bid < qblk
    bid = bid.astype(F32)
    kmean = kmean_ref[...]
    for hh in range(2):
        qf = jnp.where((row >= hh * HEAD_DIM) & (row < (hh + 1) * HEAD_DIM), qt, 0.0)
        for qb in range(ATTN_CHUNK):
            qz_ref[2 * qb + hh] = qf[:, qb * ab:(qb + 1) * ab].astype(BF16)
        km = jnp.where((lane >= hh * HEAD_DIM) & (lane < (hh + 1) * HEAD_DIM), kmean, 0.0)
        gate = jnp.dot(km, qf, preferred_element_type=F32, precision=lax.Precision.HIGHEST)
        gate = jnp.where(past, gate, NEG)
        sel = jnp.zeros_like(gate)
        for _ in range(MOBA_TOPK):
            top = jnp.max(gate, axis=0, keepdims=True)
            first = jnp.min(jnp.where(gate == top, bid, float(n_blocks)), axis=0, keepdims=True)
            hit = bid == first
            sel = jnp.where(hit, 1.0, sel)
            gate = jnp.where(hit, BELOW_NEG, gate)
        bias_ref[hh] = jnp.where(past, jnp.where(sel > 0.0, 0.0, NEG), NEG)

    none = lambda w: None

    def picked(c, blk):
        qb, hh = c // 2, c % 2
        return bias_ref[hh, pl.ds(blk, 1), qb * ab:(qb + 1) * ab]

    def full_bias(c, first):
        return none, lambda w: picked(c, first + w)

    def diag_bias(c):
        qb = chains[c][0]
        return ((lambda w: tri_ref[...] if w == qb else None),
                (lambda w: None if w == qb else picked(c, g * ATTN_CHUNK + w)))

    ml = _flash_tile(g, k_ref, vt_ref, chains, full_bias, diag_bias, qz_ref, s_ref, mx_ref, acc_ref)

    for qb in range(ATTN_CHUNK):
        outs = []
        for hh in range(2):
            o = acc_ref[2 * qb + hh] / ml[2 * qb + hh][1]
            outs.append(o * lax.rsqrt(jnp.mean(o * o, axis=0, keepdims=True) + EPS) * g_ref[...])
        o_ref[0, qb * ab:(qb + 1) * ab, :] = jnp.concatenate(outs, axis=0).T.astype(o_ref.dtype)


def _moba_attention(qt, k, vt, norm_g):
    bsz, nb, wq, ab = qt.shape
    s = k.shape[1]
    pair = 2 * HEAD_DIM
    tq = ATTN_CHUNK * ab
    kern = functools.partial(_moba_kernel, n_blocks=nb)
    in_specs = [
        _const_spec((HEAD_DIM, 1)),
        pl.BlockSpec((1, ATTN_CHUNK, pair, ab), lambda b, h, i: (b, i, h, 0)),
        pl.BlockSpec((1, s, pair), lambda b, h, i: (b, 0, h)),
        pl.BlockSpec((1, nb, pair, ab), lambda b, h, i: (b, 0, h, 0)),
    ]
    return pl.pallas_call(
        kern, grid=(bsz, wq // pair, nb // ATTN_CHUNK), in_specs=in_specs,
        out_specs=pl.BlockSpec((1, tq, pair), lambda b, h, i: (b, i, h)),
        out_shape=jax.ShapeDtypeStruct((bsz, s, wq), BF16),
        scratch_shapes=[pltpu.VMEM((nb, pair), F32), pltpu.VMEM((2, nb, tq), F32)]
        + _flash_scratch(2 * ATTN_CHUNK, HEAD_DIM),
        compiler_params=_params("parallel", "parallel", "arbitrary"), name="moba_attention",
    )(norm_g.reshape(HEAD_DIM, 1), qt, k, vt)


def _out_mlp_kernel(x_ref, ys_ref, yd_ref, ym_ref, mod_ref, g2_ref, wo_ref, w1_ref, w2_ref, fg_ref,
                    o_ref, *, final_norm, ff_chunk):
    x = x_ref[0]
    mod = mod_ref[0]
    ws = ys_ref.shape[2]
    wd = yd_ref.shape[2]
    mix = (jnp.dot(ys_ref[0], wo_ref[0:ws], preferred_element_type=F32)
           + jnp.dot(yd_ref[0], wo_ref[ws:ws + wd], preferred_element_type=F32)
           + jnp.dot(ym_ref[0], wo_ref[ws + wd:], preferred_element_type=F32))
    x = x + mod[2:3] * mix
    h = (_rmsnorm_rows(x, g2_ref[...]) * (1.0 + mod[4:5]) + mod[3:4]).astype(BF16)
    acc = jnp.zeros_like(x)
    for c in range(w1_ref.shape[1] // ff_chunk):
        a = jnp.dot(h, w1_ref[:, c * ff_chunk:(c + 1) * ff_chunk], preferred_element_type=F32)
        a = jnp.square(jnp.maximum(a, 0.0)).astype(BF16)
        acc = acc + jnp.dot(a, w2_ref[c * ff_chunk:(c + 1) * ff_chunk, :], preferred_element_type=F32)
    x = x + mod[5:6] * acc
    if final_norm:
        x = _rmsnorm_rows(x, fg_ref[...])
    o_ref[0] = x


def _out_mlp(x, ys, yd, ym, mod, g2, wo, w1, w2, final_g, final_norm):
    bsz, s, d = x.shape
    tm = TOKEN_TILE
    kern = functools.partial(_out_mlp_kernel, final_norm=final_norm, ff_chunk=d)
    tok = lambda w: pl.BlockSpec((1, tm, w), lambda b, i: (b, i, 0))
    in_specs = [
        tok(d), tok(ys.shape[2]), tok(yd.shape[2]), tok(ym.shape[2]),
        pl.BlockSpec((1, 6, d), lambda b, i: (b, 0, 0)),
        _const_spec((1, d)), _const_spec(wo.shape), _const_spec(w1.shape), _const_spec(w2.shape),
        _const_spec((1, d)),
    ]
    return pl.pallas_call(
        kern, grid=(bsz, s // tm), in_specs=in_specs, out_specs=tok(d),
        out_shape=jax.ShapeDtypeStruct((bsz, s, d), F32),
        compiler_params=_params("parallel", "parallel"), name="out_mlp",
    )(x, ys, yd, ym, mod, g2, wo, w1, w2, final_g)


def kernel(x, c, positions, norm1_g, norm2_g, w_ada, b_ada, w_in, w_out, ssm_a_re, ssm_a_im, ssm_log_dt, ssm_b_re, ssm_b_im, ssm_c_re, ssm_c_im, ssm_d, ssm_glu_w, ssm_glu_b, ssm_norm_g, diff_lq1, diff_lk1, diff_lq2, diff_lk2, diff_subln_g, moba_norm_g, mlp_w1, mlp_w2, final_g):
    bsz, s, d = x.shape
    depth = w_in.shape[0]
    w_ssm = ssm_d.shape[1]
    w_diff = DIFF_HEADS * 2 * HEAD_DIM
    w_moba = MOBA_HEADS * HEAD_DIM
    assert s % TOKEN_TILE == 0 and TOKEN_TILE % ATTN_BLOCK == 0 and s % (ATTN_CHUNK * ATTN_BLOCK) == 0
    assert s % SSM_TILE == 0 and SSM_TILE % (SSM_CHUNK * SUBLANES) == 0
    assert w_in.shape[2] == w_ssm + 3 * w_diff + 3 * w_moba

    cos_t, sin_t = _rope_tables(positions)
    mod_all = _ada_mod(c, w_ada, b_ada).reshape(depth, bsz, 6, d)
    sw1, sw2, sw3, scoef = _ssm_prep(ssm_a_re, ssm_a_im, ssm_log_dt, ssm_b_re, ssm_b_im, ssm_c_re, ssm_c_im)

    o_dq = w_ssm
    o_dk = o_dq + w_diff
    o_dv = o_dk + w_diff
    o_mq = o_dv + w_diff
    o_mk = o_mq + w_moba
    o_mv = o_mk + w_moba
    widths = (w_diff, w_diff, w_moba, w_moba, w_diff, w_moba)

    for l in range(depth):
        wl = w_in[l]
        wu = wl[:, :o_dq].astype(BF16)
        wt = jnp.concatenate([wl[:, o_dq:o_dk], wl[:, o_dk:o_dv], wl[:, o_mq:o_mk], wl[:, o_mk:o_mv],
                              wl[:, o_dv:o_mq], wl[:, o_mv:]], axis=1).T.astype(BF16)
        mod = mod_all[l]
        u, dq, dk, dv, mq, mk, mv = _projection(x, mod, norm1_g[l][None], wu, wt, cos_t, sin_t, widths)

        y_ssm = _ssm(u, sw1[l], sw2[l], sw3[l], scoef[l], ssm_d[l][None], ssm_glu_w[l].astype(BF16),
                     ssm_glu_b[l][None], ssm_norm_g[l][None])

        lam_init = 0.8 - 0.6 * math.exp(-0.3 * l)
        lam_params = jnp.stack([diff_lq1[l], diff_lk1[l], diff_lq2[l], diff_lk2[l]]).astype(F32)
        y_diff = _diff_attention(dq, dk, dv, lam_params, diff_subln_g[l], lam_init)
        y_moba = _moba_attention(mq, mk, mv, moba_norm_g[l])

        x = _out_mlp(x, y_ssm, y_diff, y_moba, mod, norm2_g[l][None], w_out[l].astype(BF16),
                     mlp_w1[l].astype(BF16), mlp_w2[l].astype(BF16), final_g[None],
                     final_norm=(l == depth - 1))
    return x
```

```python
import functools
import math

import jax
import jax.numpy as jnp
from jax import lax
from jax.experimental import pallas as pl
from jax.experimental.pallas import tpu as pltpu

F32 = jnp.float32
BF16 = jnp.bfloat16

SSM_GROUP = 16
SSM_STATE = 64
HEAD_DIM = 64
DIFF_HEADS = 4
MOBA_HEADS = 4
MOBA_BLOCK = 256
MOBA_TOPK = 3
ROPE_THETA = 500000.0
ROPE_HALF = 8
EPS = 1e-6
LOG2_E = math.log2(math.e)
NEG = -1e30
BELOW_NEG = -3e38

SUBLANES = 8
LANES = 128
TOKEN_TILE = 512
SSM_CHUNK = 4
SSM_TILE = 2048
ATTN_BLOCK = MOBA_BLOCK
ATTN_CHUNK = 4
VMEM_LIMIT = 56 * 1024 * 1024


def _params(*sem):
    return pltpu.CompilerParams(dimension_semantics=sem, vmem_limit_bytes=VMEM_LIMIT)


def _const_spec(shape):
    nd = len(shape)
    return pl.BlockSpec(shape, lambda *_: (0,) * nd)


def _rope_kernel(inv_ref, pos_ref, cos_ref, sin_ref):
    ang = inv_ref[...] * pos_ref[0].astype(F32)
    cos_ref[0] = jnp.cos(ang)
    sin_ref[0] = jnp.sin(ang)


def _rope_tables(positions):
    bsz, s = positions.shape
    rot = 2 * ROPE_HALF
    inv = (ROPE_THETA ** (-jnp.arange(0, rot, 2, dtype=F32) / rot)).reshape(ROPE_HALF, 1)
    out = jax.ShapeDtypeStruct((bsz, ROPE_HALF, s), F32)
    return pl.pallas_call(
        _rope_kernel,
        grid=(bsz,),
        in_specs=[_const_spec((ROPE_HALF, 1)),
                  pl.BlockSpec((1, 1, s), lambda b: (b, 0, 0))],
        out_specs=[pl.BlockSpec((1, ROPE_HALF, s), lambda b: (b, 0, 0))] * 2,
        out_shape=[out, out],
        compiler_params=_params("parallel"),
        name="rope_tables",
    )(inv, positions.reshape(bsz, 1, s))


def _ada_kernel(c_ref, w_ref, b_ref, o_ref):
    c = c_ref[...]
    sc = c * jax.nn.sigmoid(c)
    o_ref[0] = jnp.dot(sc, w_ref[0], preferred_element_type=F32,
                       precision=lax.Precision.HIGHEST) + b_ref[0]


def _ada_mod(c, w_ada, b_ada):
    depth, d, d6 = w_ada.shape
    bsz = c.shape[0]
    nchunk = d6 // d
    return pl.pallas_call(
        _ada_kernel,
        grid=(depth, nchunk),
        in_specs=[_const_spec((bsz, d)),
                  pl.BlockSpec((1, d, d), lambda l, j: (l, 0, j)),
                  pl.BlockSpec((1, 1, d), lambda l, j: (l, 0, j))],
        out_specs=pl.BlockSpec((1, bsz, d), lambda l, j: (l, 0, j)),
        out_shape=jax.ShapeDtypeStruct((depth, bsz, d6), F32),
        compiler_params=_params("parallel", "parallel"),
        name="ada_mod",
    )(c, w_ada, b_ada.reshape(depth, 1, d6))


def _ssm_prep_kernel(are_ref, aim_ref, ldt_ref, bre_ref, bim_ref,
                     bbr_ref, bbi_ref, p1r_ref, p1i_ref, pcr_ref, pci_ref):
    ar = are_ref[...]
    ai = aim_ref[...]
    dt = jnp.exp(ldt_ref[...])
    mag = jnp.exp(ar * dt)
    abr = mag * jnp.cos(ai * dt)
    abi = mag * jnp.sin(ai * dt)
    den = ar * ar + ai * ai
    nr = abr - 1.0
    ni = abi
    kr = (nr * ar + ni * ai) / den
    ki = (ni * ar - nr * ai) / den
    br = bre_ref[...]
    bi = bim_ref[...]
    bbr_ref[...] = kr[None] * br - ki[None] * bi
    bbi_ref[...] = kr[None] * bi + ki[None] * br
    pr, pi = abr, abi
    p1r_ref[0] = pr
    p1i_ref[0] = pi
    for j in range(1, SSM_CHUNK):
        pr, pi = pr * abr - pi * abi, pr * abi + pi * abr
        p1r_ref[j] = pr
        p1i_ref[j] = pi
    cr, ci = pr, pi
    pcr_ref[0] = cr
    pci_ref[0] = ci
    for j in range(1, SUBLANES):
        pr, pi = pr * cr - pi * ci, pr * ci + pi * cr
        pcr_ref[j] = pr
        pci_ref[j] = pi


def _ssm_prep(a_re, a_im, log_dt, b_re, b_im, c_re, c_im):
    depth, g, p = a_re.shape
    cch = b_re.shape[-1]
    gg = depth * g
    bt = lambda b: jnp.transpose(b, (3, 0, 1, 2)).reshape(cch, gg, p)
    shapes = ([jax.ShapeDtypeStruct((cch, gg, p), F32)] * 2 + [jax.ShapeDtypeStruct((SSM_CHUNK, gg, p), F32)] * 2
              + [jax.ShapeDtypeStruct((SUBLANES, gg, p), F32)] * 2)
    outs = pl.pallas_call(
        _ssm_prep_kernel, out_shape=shapes, name="ssm_prep",
    )(a_re.reshape(gg, p), a_im.reshape(gg, p), log_dt.reshape(gg, 1), bt(b_re), bt(b_im))
    per_layer = [x.reshape(x.shape[0], depth, g, p) for x in outs]
    a1, compact, coef = jax.vmap(_ssm_matrices, in_axes=(1, 1, 1, 1, 1, 1, 0, 0))(*per_layer, c_re, c_im)
    w1, w2, w3 = _ssm_expand(a1, compact, g, cch, p)
    return w1, w2, w3, coef


def _ssm_expand_kernel(a1_ref, cm_ref, w1_ref, w2_ref, w3_ref, *, groups, chans, states):
    hi = lax.Precision.HIGHEST
    nl = SSM_CHUNK
    lw = nl * groups * chans
    n = groups * states
    io = lambda shape, dim: lax.broadcasted_iota(jnp.int32, shape, dim)
    grp_tgc = lambda i: (i // chans) % groups
    grp_gp = lambda i: i // states
    col = io((nl * chans, lw), 1)
    e1 = (io((nl * chans, lw), 0) == (col // (groups * chans)) * chans + col % chans).astype(F32)
    e2 = (io((states, n), 0) == io((states, n), 1) % states).astype(F32)
    a1 = a1_ref[0]
    m1 = grp_tgc(io((lw, lw), 0)) == grp_tgc(io((lw, lw), 1))
    w1_ref[0] = jnp.where(m1, jnp.dot(a1, e1, preferred_element_type=F32, precision=hi), 0.0).astype(BF16)
    m2 = grp_tgc(io((lw, n), 0)) == grp_gp(io((lw, n), 1))
    m3 = grp_gp(io((n, lw), 0)) == grp_tgc(io((n, lw), 1))
    for ri in range(2):
        x = jnp.dot(cm_ref[0, ri], e2, preferred_element_type=F32, precision=hi)
        w2_ref[0, :, ri * n:(ri + 1) * n] = jnp.where(m2, x, 0.0).astype(BF16)
        y = lax.dot_general(e2, cm_ref[0, 2 + ri], (((0,), (1,)), ((), ())),
                            preferred_element_type=F32, precision=hi)
        w3_ref[0, ri * n:(ri + 1) * n, :] = jnp.where(m3, y, 0.0).astype(BF16)


def _ssm_expand(a1, compact, groups, chans, states):
    depth = a1.shape[0]
    lw = SSM_CHUNK * groups * chans
    n = groups * states
    kern = functools.partial(_ssm_expand_kernel, groups=groups, chans=chans, states=states)
    return pl.pallas_call(
        kern, grid=(depth,),
        in_specs=[pl.BlockSpec((1,) + a1.shape[1:], lambda l: (l, 0, 0)),
                  pl.BlockSpec((1,) + compact.shape[1:], lambda l: (l, 0, 0, 0))],
        out_specs=[pl.BlockSpec((1, lw, lw), lambda l: (l, 0, 0)),
                   pl.BlockSpec((1, lw, 2 * n), lambda l: (l, 0, 0)),
                   pl.BlockSpec((1, 2 * n, lw), lambda l: (l, 0, 0))],
        out_shape=[jax.ShapeDtypeStruct((depth, lw, lw), BF16), jax.ShapeDtypeStruct((depth, lw, 2 * n), BF16),
                   jax.ShapeDtypeStruct((depth, 2 * n, lw), BF16)],
        compiler_params=_params("parallel"), name="ssm_expand",
    )(a1, compact)


def _ssm_matrices(bbr, bbi, p1r, p1i, pcr, pci, c_re, c_im):
    cch, g, p = bbr.shape
    w = g * cch
    n = g * p
    nl = SSM_CHUNK
    hi = lax.Precision.HIGHEST
    ar = jnp.concatenate([jnp.ones((1, g, p), F32), p1r], axis=0)
    ai = jnp.concatenate([jnp.zeros((1, g, p), F32), p1i], axis=0)
    car = c_re[None] * ar[:, :, None, :] - c_im[None] * ai[:, :, None, :]
    cai = c_re[None] * ai[:, :, None, :] + c_im[None] * ar[:, :, None, :]
    kk = (jnp.einsum('tgcp,dgp->tgcd', car[:nl], bbr, precision=hi)
          - jnp.einsum('tgcp,dgp->tgcd', cai[:nl], bbi, precision=hi))
    zk = jnp.zeros_like(kk[0])
    ksel = jnp.stack([jnp.stack([kk[t2 - t] if t2 >= t else zk for t2 in range(nl)])
                      for t in range(nl)])
    a1 = jnp.transpose(ksel, (0, 2, 4, 1, 3)).reshape(nl * w, nl * cch)
    rev_r = jnp.stack([ar[nl - 1 - t] for t in range(nl)])
    rev_i = jnp.stack([ai[nl - 1 - t] for t in range(nl)])
    abr_ = rev_r[:, None] * bbr[None] - rev_i[:, None] * bbi[None]
    abi_ = rev_r[:, None] * bbi[None] + rev_i[:, None] * bbr[None]
    rows2 = lambda x: jnp.transpose(x, (0, 2, 1, 3)).reshape(nl * w, p)
    compact = jnp.stack([rows2(abr_), rows2(abi_),
                         car[1:].reshape(nl * w, p), -cai[1:].reshape(nl * w, p)])
    pcr = pcr.reshape(SUBLANES, n)
    pci = pci.reshape(SUBLANES, n)
    row = jnp.arange(SUBLANES)[:, None]
    tabs = []
    for shift in (1, 2, 4):
        keep = row >= shift
        tabs.append(jnp.where(keep, pcr[shift - 1][None], 0.0))
        tabs.append(jnp.where(keep, pci[shift - 1][None], 0.0))
    tabs += [pcr, pci]
    return a1, compact, jnp.stack(tabs)


def _rmsnorm_rows(x, g):
    return x * lax.rsqrt(jnp.mean(x * x, axis=-1, keepdims=True) + EPS) * g


def _proj_kernel(x_ref, mod_ref, g_ref, wu_ref, wt_ref, cos_ref, sin_ref,
                 u_ref, dq_ref, dk_ref, dv_ref, mq_ref, mk_ref, mv_ref, *, n_rot, widths):
    wdq, wdk, wmq, wmk, wdv, wmv = widths
    x = x_ref[0]
    mod = mod_ref[0]
    h = _rmsnorm_rows(x, g_ref[...]) * (1.0 + mod[1:2]) + mod[0:1]
    hb = h.astype(BF16)
    u = jnp.dot(hb, wu_ref[...], preferred_element_type=F32)
    for j in range(u_ref.shape[1]):
        u_ref[0, j] = u[:, j * LANES:(j + 1) * LANES]
    pt = lax.dot_general(wt_ref[...], hb, (((1,), (1,)), ((), ())), preferred_element_type=F32)
    tm = pt.shape[1]
    rot = pt[:n_rot].reshape(n_rot // HEAD_DIM, HEAD_DIM, tm)
    cos = cos_ref[0][None]
    sin = sin_ref[0][None]
    x1 = rot[:, 0:ROPE_HALF]
    x2 = rot[:, ROPE_HALF:2 * ROPE_HALF]
    rot = jnp.concatenate([x1 * cos - x2 * sin, x2 * cos + x1 * sin, rot[:, 2 * ROPE_HALF:]], axis=1)
    rot = rot.reshape(n_rot, tm)
    qscale = HEAD_DIM ** -0.5 * LOG2_E
    o = 0
    dq = (rot[o:o + wdq] * qscale).astype(BF16); o += wdq
    dk_ref[0] = rot[o:o + wdk].T.astype(BF16); o += wdk
    mq = (rot[o:o + wmq] * qscale).astype(BF16); o += wmq
    mk_ref[0] = rot[o:o + wmk].T.astype(BF16); o += wmk
    dv = pt[o:o + wdv].astype(BF16); o += wdv
    mv = pt[o:o + wmv].astype(BF16)
    for j in range(tm // ATTN_BLOCK):
        blk = slice(j * ATTN_BLOCK, (j + 1) * ATTN_BLOCK)
        dq_ref[0, j] = dq[:, blk]
        dv_ref[0, j] = dv[:, blk]
        mq_ref[0, j] = mq[:, blk]
        mv_ref[0, j] = mv[:, blk]


def _projection(x, mod, g, wu, wt, cos_t, sin_t, widths):
    bsz, s, d = x.shape
    tm = TOKEN_TILE
    ns = s // tm
    wdq, wdk, wmq, wmk, wdv, wmv = widths
    n_rot = wdq + wdk + wmq + wmk
    ab = ATTN_BLOCK
    nab = tm // ab
    kern = functools.partial(_proj_kernel, n_rot=n_rot, widths=widths)
    tiles_t = lambda w: jax.ShapeDtypeStruct((bsz, s // ab, w, ab), BF16)
    out_shape = [
        jax.ShapeDtypeStruct((bsz, wu.shape[1] // LANES, s, LANES), F32),
        tiles_t(wdq),
        jax.ShapeDtypeStruct((bsz, s, wdk), BF16),
        tiles_t(wdv),
        tiles_t(wmq),
        jax.ShapeDtypeStruct((bsz, s, wmk), BF16),
        tiles_t(wmv),
    ]
    spec_t = lambda w: pl.BlockSpec((1, nab, w, ab), lambda b, i: (b, i, 0, 0))
    out_specs = [
        pl.BlockSpec((1, wu.shape[1] // LANES, tm, LANES), lambda b, i: (b, 0, i, 0)),
        spec_t(wdq),
        pl.BlockSpec((1, tm, wdk), lambda b, i: (b, i, 0)),
        spec_t(wdv),
        spec_t(wmq),
        pl.BlockSpec((1, tm, wmk), lambda b, i: (b, i, 0)),
        spec_t(wmv),
    ]
    in_specs = [
        pl.BlockSpec((1, tm, d), lambda b, i: (b, i, 0)),
        pl.BlockSpec((1, 6, d), lambda b, i: (b, 0, 0)),
        _const_spec((1, d)),
        _const_spec(wu.shape),
        _const_spec(wt.shape),
        pl.BlockSpec((1, ROPE_HALF, tm), lambda b, i: (b, 0, i)),
        pl.BlockSpec((1, ROPE_HALF, tm), lambda b, i: (b, 0, i)),
    ]
    return pl.pallas_call(
        kern, grid=(bsz, ns), in_specs=in_specs, out_specs=out_specs, out_shape=out_shape,
        compiler_params=_params("parallel", "parallel"), name="projection",
    )(x, mod, g, wu, wt, cos_t, sin_t)


def _gelu_tanh(y):
    return 0.5 * y * (1.0 + jnp.tanh(math.sqrt(2.0 / math.pi) * (y + 0.044715 * (y * y * y))))


def _ssm_kernel(u_ref, w1_ref, w2_ref, w3_ref, coef_ref, d_ref, gw_ref, gb_ref, ng_ref,
                y_ref, st_ref, yt_ref, carry_ref, *, n_state):
    n = n_state
    nl = SSM_CHUNK

    @pl.when(pl.program_id(1) == 0)
    def _():
        carry_ref[...] = jnp.zeros_like(carry_ref)

    ns, tm = u_ref.shape[1], u_ref.shape[2]
    w = ns * LANES
    mc = tm // nl
    uc = jnp.concatenate([u_ref[0, j, pl.ds(t, mc, stride=nl), :] for t in range(nl) for j in range(ns)],
                         axis=1).astype(BF16)
    st_ref[...] = jnp.dot(uc, w2_ref[...], preferred_element_type=F32)

    def cmul_add(xr, xi, ar, ai, sr, si):
        return xr + (ar * sr - ai * si), xi + (ar * si + ai * sr)

    first_row = lax.broadcasted_iota(jnp.int32, (SUBLANES, n), 0) == 0

    def group(r, carry):
        cr, ci = carry
        rows = pl.ds(pl.multiple_of(r * SUBLANES, SUBLANES), SUBLANES)
        xr = st_ref[rows, 0:n]
        xi = st_ref[rows, n:2 * n]
        for lvl, shift in enumerate((1, 2, 4)):
            sr = pltpu.roll(xr, shift, axis=0)
            si = pltpu.roll(xi, shift, axis=0)
            xr, xi = cmul_add(xr, xi, coef_ref[2 * lvl], coef_ref[2 * lvl + 1], sr, si)
        xr, xi = cmul_add(xr, xi, coef_ref[6], coef_ref[7], cr, ci)
        st_ref[rows, 0:n] = jnp.where(first_row, cr, pltpu.roll(xr, 1, axis=0))
        st_ref[rows, n:2 * n] = jnp.where(first_row, ci, pltpu.roll(xi, 1, axis=0))
        last = SUBLANES - 1
        return (jnp.broadcast_to(xr[last:last + 1], xr.shape),
                jnp.broadcast_to(xi[last:last + 1], xi.shape))

    cr, ci = lax.fori_loop(0, mc // SUBLANES, group, (carry_ref[0], carry_ref[1]))
    carry_ref[0] = cr
    carry_ref[1] = ci

    yc = (jnp.dot(uc, w1_ref[...], preferred_element_type=F32)
          + jnp.dot(st_ref[...].astype(BF16), w3_ref[...], preferred_element_type=F32))
    for t in range(nl):
        for j in range(ns):
            yt_ref[j, pl.ds(t, mc, stride=nl), :] = yc[:, t * w + j * LANES:t * w + (j + 1) * LANES]
    u = jnp.concatenate([u_ref[0, j] for j in range(ns)], axis=1)
    y = jnp.concatenate([yt_ref[j] for j in range(ns)], axis=1) + d_ref[...] * u
    y = _gelu_tanh(y)
    z = jnp.dot(y.astype(BF16), gw_ref[...], preferred_element_type=F32) + gb_ref[...]
    y = y * jax.nn.sigmoid(z)
    y_ref[0] = _rmsnorm_rows(y, ng_ref[...]).astype(y_ref.dtype)


def _ssm(u, w1, w2, w3, coef, d_skip, glu_w, glu_b, norm_g):
    bsz, ns, s, _ = u.shape
    w = ns * LANES
    n = coef.shape[-1]
    tm = SSM_TILE
    kern = functools.partial(_ssm_kernel, n_state=n)
    in_specs = [
        pl.BlockSpec((1, ns, tm, LANES), lambda b, i: (b, 0, i, 0)),
        _const_spec(w1.shape), _const_spec(w2.shape), _const_spec(w3.shape), _const_spec(coef.shape),
        _const_spec((1, w)), _const_spec((w, w)), _const_spec((1, w)), _const_spec((1, w)),
    ]
    return pl.pallas_call(
        kern, grid=(bsz, s // tm), in_specs=in_specs,
        out_specs=pl.BlockSpec((1, tm, w), lambda b, i: (b, i, 0)),
        out_shape=jax.ShapeDtypeStruct((bsz, s, w), BF16),
        scratch_shapes=[pltpu.VMEM((tm // SSM_CHUNK, 2 * n), F32), pltpu.VMEM((ns, tm, LANES), F32),
                        pltpu.VMEM((2, SUBLANES, n), F32)],
        compiler_params=_params("parallel", "arbitrary"), name="ssm",
    )(u, w1, w2, w3, coef, d_skip, glu_w, glu_b, norm_g)


def _row_groups(x):
    rows, lanes = x.shape
    return x.reshape(4, rows // (4 * SUBLANES), SUBLANES, lanes)


def _score_stage(k_ref, first_blk, n_gran, qz, tile_bias, s_ref, mx_ref):
    ab = ATTN_BLOCK
    for w in range(n_gran):
        kb = k_ref[0, pl.ds(pl.multiple_of((first_blk + w) * ab, ab), ab), :]
        s = jnp.dot(kb, qz, preferred_element_type=F32)
        tb = tile_bias(w)
        if tb is not None:
            s = s + tb
        s_ref[w * ab:(w + 1) * ab, :] = s
        mx_ref[w] = jnp.max(jnp.max(_row_groups(s), axis=1), axis=0)


def _softmax_stage(s_ref, mx_ref, vt_ref, first_blk, n_gran, v_rows, row_bias, m, l, acc_ref):
    ab = ATTN_BLOCK
    rbs = [row_bias(w) for w in range(n_gran)]
    raw = mx_ref[0]
    seen = raw if rbs[0] is None else raw + rbs[0]
    for w in range(1, n_gran):
        raw = jnp.maximum(raw, mx_ref[w])
        seen = jnp.maximum(seen, mx_ref[w] if rbs[w] is None else mx_ref[w] + rbs[w])
    m_new = jnp.maximum(m, jnp.max(seen, axis=0, keepdims=True))
    shift = jnp.where(m_new > 0.5 * NEG, m_new, jnp.max(raw, axis=0, keepdims=True))
    alpha = jnp.exp2(m - m_new)
    psum = None
    pv = None
    for w in range(n_gran):
        p = jnp.exp2(s_ref[w * ab:(w + 1) * ab, :] - (shift if rbs[w] is None else shift - rbs[w]))
        part = jnp.sum(jnp.sum(_row_groups(p), axis=1), axis=0)
        psum = part if psum is None else psum + part
        d = jnp.dot(vt_ref[0, first_blk + w, v_rows, :], p.astype(BF16), preferred_element_type=F32)
        pv = d if pv is None else pv + d
    acc_ref[...] = alpha * acc_ref[...] + pv
    return m_new, alpha * l + jnp.sum(psum, axis=0, keepdims=True)


def _flash_tile(g, k_ref, vt_ref, chains, full_bias, diag_bias, qz_ref, s_ref, mx_ref, acc_ref):
    nc = len(chains)
    t = s_ref.shape[-1]
    acc_ref[...] = jnp.zeros(acc_ref.shape, F32)
    neg = jnp.full((1, t), NEG, F32)
    zero = jnp.zeros((1, t), F32)

    def score(c, first, n_gran, bias):
        _score_stage(k_ref, first, n_gran, qz_ref[c], bias[0], s_ref.at[c], mx_ref.at[c])

    def softmax(c, first, n_gran, bias, m, l):
        return _softmax_stage(s_ref.at[c], mx_ref.at[c], vt_ref, first, n_gran, chains[c][1], bias[1],
                              m, l, acc_ref.at[c])

    score(0, 0, ATTN_CHUNK, full_bias(0, 0))

    def step(i, carry):
        first = i * ATTN_CHUNK
        out = []
        for c in range(nc):
            if c + 1 < nc:
                score(c + 1, first, ATTN_CHUNK, full_bias(c + 1, first))
            else:
                score(0, first + ATTN_CHUNK, ATTN_CHUNK, full_bias(0, first + ATTN_CHUNK))
            out += softmax(c, first, ATTN_CHUNK, full_bias(c, first), carry[2 * c], carry[2 * c + 1])
        return tuple(out)

    carry = lax.fori_loop(0, g, step, (neg, zero) * nc)
    first = g * ATTN_CHUNK
    out = []
    score(0, first, chains[0][0] + 1, diag_bias(0))
    for c in range(nc):
        if c + 1 < nc:
            score(c + 1, first, chains[c + 1][0] + 1, diag_bias(c + 1))
        out.append(softmax(c, first, chains[c][0] + 1, diag_bias(c), carry[2 * c], carry[2 * c + 1]))
    return out


def _flash_scratch(nc, dv):
    ab = ATTN_BLOCK
    return [pltpu.VMEM((nc, 2 * HEAD_DIM, ab), BF16),
            pltpu.VMEM((nc, ATTN_CHUNK * ab, ab), F32),
            pltpu.VMEM((nc, ATTN_CHUNK, SUBLANES, ab), F32),
            pltpu.VMEM((nc, dv, ab), F32),
            pltpu.VMEM((ab, ab), F32)]


def _causal_bias(t):
    key = lax.broadcasted_iota(jnp.int32, (t, t), 0)
    qry = lax.broadcasted_iota(jnp.int32, (t, t), 1)
    return jnp.where(key <= qry, 0.0, NEG)


def _diff_kernel(lam_ref, g_ref, qt_ref, k_ref, vt_ref, o_ref, qz_ref, s_ref, mx_ref, acc_ref, tri_ref,
                 *, lam_init):
    g = pl.program_id(2)
    ab = ATTN_BLOCK
    chains = [(qb, slice(None)) for qb in range(ATTN_CHUNK) for _ in range(2)]
    tri_ref[...] = _causal_bias(ab)
    row = lax.broadcasted_iota(jnp.int32, (2 * HEAD_DIM, ab), 0)
    for qb in range(ATTN_CHUNK):
        qt = qt_ref[0, qb].astype(F32)
        qz_ref[2 * qb] = jnp.where(row < HEAD_DIM, qt, 0.0).astype(BF16)
        qz_ref[2 * qb + 1] = jnp.where(row >= HEAD_DIM, qt, 0.0).astype(BF16)

    none = lambda w: None

    def diag_bias(c):
        qb = chains[c][0]
        return (lambda w: tri_ref[...] if w == qb else None), none

    ml = _flash_tile(g, k_ref, vt_ref, chains, lambda c, first: (none, none), diag_bias,
                     qz_ref, s_ref, mx_ref, acc_ref)

    lp = lam_ref[...]
    lam = (jnp.exp(jnp.sum(lp[0:1] * lp[1:2], axis=1, keepdims=True))
           - jnp.exp(jnp.sum(lp[2:3] * lp[3:4], axis=1, keepdims=True)) + lam_init)
    for qb in range(ATTN_CHUNK):
        o = acc_ref[2 * qb] / ml[2 * qb][1] - lam * (acc_ref[2 * qb + 1] / ml[2 * qb + 1][1])
        o = o * lax.rsqrt(jnp.mean(o * o, axis=0, keepdims=True) + EPS) * g_ref[...]
        o_ref[0, qb * ab:(qb + 1) * ab, :] = (o * (1.0 - lam_init)).T.astype(o_ref.dtype)


def _diff_attention(qt, k, vt, lam_params, subln_g, lam_init):
    bsz, nb, wq, ab = qt.shape
    s = k.shape[1]
    dv = 2 * HEAD_DIM
    nh = wq // dv
    tq = ATTN_CHUNK * ab
    kern = functools.partial(_diff_kernel, lam_init=lam_init)
    in_specs = [
        _const_spec(lam_params.shape),
        _const_spec((dv, 1)),
        pl.BlockSpec((1, ATTN_CHUNK, dv, ab), lambda b, h, i: (b, i, h, 0)),
        pl.BlockSpec((1, s, dv), lambda b, h, i: (b, 0, h)),
        pl.BlockSpec((1, nb, dv, ab), lambda b, h, i: (b, 0, h, 0)),
    ]
    return pl.pallas_call(
        kern, grid=(bsz, nh, nb // ATTN_CHUNK), in_specs=in_specs,
        out_specs=pl.BlockSpec((1, tq, dv), lambda b, h, i: (b, i, h)),
        out_shape=jax.ShapeDtypeStruct((bsz, s, wq), BF16),
        scratch_shapes=_flash_scratch(2 * ATTN_CHUNK, dv),
        compiler_params=_params("parallel", "parallel", "arbitrary"), name="diff_attention",
    )(lam_params, subln_g.reshape(dv, 1), qt, k, vt)


def _moba_kernel(g_ref, qt_ref, k_ref, vt_ref, o_ref, kmean_ref, bias_ref, qz_ref, s_ref, mx_ref, acc_ref,
                 tri_ref, *, n_blocks):
    g = pl.program_id(2)
    ab = ATTN_BLOCK
    tq = ATTN_CHUNK * ab
    pair = 2 * HEAD_DIM

    @pl.when(g == 0)
    def _():
        def mean_block(j, c):
            kb = k_ref[0, pl.ds(pl.multiple_of(j * ab, ab), ab), :].astype(F32)
            kmean_ref[pl.ds(j, 1), :] = jnp.mean(kb, axis=0, keepdims=True)
            return c
        lax.fori_loop(0, n_blocks, mean_block, 0)

    tri_ref[...] = _causal_bias(ab)
    chains = [(qb, slice(hh * HEAD_DIM, (hh + 1) * HEAD_DIM)) for qb in range(ATTN_CHUNK) for hh in range(2)]
    qt = jnp.concatenate([qt_ref[0, qb] for qb in range(ATTN_CHUNK)], axis=1).astype(F32)
    row = lax.broadcasted_iota(jnp.int32, (pair, tq), 0)
    lane = lax.broadcasted_iota(jnp.int32, (n_blocks, pair), 1)
    bid = lax.broadcasted_iota(jnp.int32, (n_blocks, tq), 0)
    qblk = g * ATTN_CHUNK + lax.broadcasted_iota(jnp.int32, (n_blocks, tq), 1) // ab
    past = bid < qblk
    bid = bid.astype(F32)
    kmean = kmean_ref[...]
    for hh in range(2):
        qf = jnp.where((row >= hh * HEAD_DIM) & (row < (hh + 1) * HEAD_DIM), qt, 0.0)
        for qb in range(ATTN_CHUNK):
            qz_ref[2 * qb + hh] = qf[:, qb * ab:(qb + 1) * ab].astype(BF16)
        km = jnp.where((lane >= hh * HEAD_DIM) & (lane < (hh + 1) * HEAD_DIM), kmean, 0.0)
        gate = jnp.dot(km, qf, preferred_element_type=F32, precision=lax.Precision.HIGHEST)
        gate = jnp.where(past, gate, NEG)
        sel = jnp.zeros_like(gate)
        for _ in range(MOBA_TOPK):
            top = jnp.max(gate, axis=0, keepdims=True)
            first = jnp.min(jnp.where(gate == top, bid, float(n_blocks)), axis=0, keepdims=True)
            hit = bid == first
            sel = jnp.where(hit, 1.0, sel)
            gate = jnp.where(hit, BELOW_NEG, gate)
        bias_ref[hh] = jnp.where(past, jnp.where(sel > 0.0, 0.0, NEG), NEG)

    none = lambda w: None

    def picked(c, blk):
        qb, hh = c // 2, c % 2
        return bias_ref[hh, pl.ds(blk, 1), qb * ab:(qb + 1) * ab]

    def full_bias(c, first):
        return none, lambda w: picked(c, first + w)

    def diag_bias(c):
        qb = chains[c][0]
        return ((lambda w: tri_ref[...] if w == qb else None),
                (lambda w: None if w == qb else picked(c, g * ATTN_CHUNK + w)))

    ml = _flash_tile(g, k_ref, vt_ref, chains, full_bias, diag_bias, qz_ref, s_ref, mx_ref, acc_ref)

    for qb in range(ATTN_CHUNK):
        outs = []
        for hh in range(2):
            o = acc_ref[2 * qb + hh] / ml[2 * qb + hh][1]
            outs.append(o * lax.rsqrt(jnp.mean(o * o, axis=0, keepdims=True) + EPS) * g_ref[...])
        o_ref[0, qb * ab:(qb + 1) * ab, :] = jnp.concatenate(outs, axis=0).T.astype(o_ref.dtype)


def _moba_attention(qt, k, vt, norm_g):
    bsz, nb, wq, ab = qt.shape
    s = k.shape[1]
    pair = 2 * HEAD_DIM
    tq = ATTN_CHUNK * ab
    kern = functools.partial(_moba_kernel, n_blocks=nb)
    in_specs = [
        _const_spec((HEAD_DIM, 1)),
        pl.BlockSpec((1, ATTN_CHUNK, pair, ab), lambda b, h, i: (b, i, h, 0)),
        pl.BlockSpec((1, s, pair), lambda b, h, i: (b, 0, h)),
        pl.BlockSpec((1, nb, pair, ab), lambda b, h, i: (b, 0, h, 0)),
    ]
    return pl.pallas_call(
        kern, grid=(bsz, wq // pair, nb // ATTN_CHUNK), in_specs=in_specs,
        out_specs=pl.BlockSpec((1, tq, pair), lambda b, h, i: (b, i, h)),
        out_shape=jax.ShapeDtypeStruct((bsz, s, wq), BF16),
        scratch_shapes=[pltpu.VMEM((nb, pair), F32), pltpu.VMEM((2, nb, tq), F32)]
        + _flash_scratch(2 * ATTN_CHUNK, HEAD_DIM),
        compiler_params=_params("parallel", "parallel", "arbitrary"), name="moba_attention",
    )(norm_g.reshape(HEAD_DIM, 1), qt, k, vt)


def _out_mlp_kernel(x_ref, ys_ref, yd_ref, ym_ref, mod_ref, g2_ref, wo_ref, w1_ref, w2_ref, fg_ref,
                    o_ref, *, final_norm, ff_chunk):
    x = x_ref[0]
    mod = mod_ref[0]
    ws = ys_ref.shape[2]
    wd = yd_ref.shape[2]
    mix = (jnp.dot(ys_ref[0], wo_ref[0:ws], preferred_element_type=F32)
           + jnp.dot(yd_ref[0], wo_ref[ws:ws + wd], preferred_element_type=F32)
           + jnp.dot(ym_ref[0], wo_ref[ws + wd:], preferred_element_type=F32))
    x = x + mod[2:3] * mix
    h = (_rmsnorm_rows(x, g2_ref[...]) * (1.0 + mod[4:5]) + mod[3:4]).astype(BF16)
    acc = jnp.zeros_like(x)
    for c in range(w1_ref.shape[1] // ff_chunk):
        a = jnp.dot(h, w1_ref[:, c * ff_chunk:(c + 1) * ff_chunk], preferred_element_type=F32)
        a = jnp.square(jnp.maximum(a, 0.0)).astype(BF16)
        acc = acc + jnp.dot(a, w2_ref[c * ff_chunk:(c + 1) * ff_chunk, :], preferred_element_type=F32)
    x = x + mod[5:6] * acc
    if final_norm:
        x = _rmsnorm_rows(x, fg_ref[...])
    o_ref[0] = x


def _out_mlp(x, ys, yd, ym, mod, g2, wo, w1, w2, final_g, final_norm):
    bsz, s, d = x.shape
    tm = TOKEN_TILE
    kern = functools.partial(_out_mlp_kernel, final_norm=final_norm, ff_chunk=d)
    tok = lambda w: pl.BlockSpec((1, tm, w), lambda b, i: (b, i, 0))
    in_specs = [
        tok(d), tok(ys.shape[2]), tok(yd.shape[2]), tok(ym.shape[2]),
        pl.BlockSpec((1, 6, d), lambda b, i: (b, 0, 0)),
        _const_spec((1, d)), _const_spec(wo.shape), _const_spec(w1.shape), _const_spec(w2.shape),
        _const_spec((1, d)),
    ]
    return pl.pallas_call(
        kern, grid=(bsz, s // tm), in_specs=in_specs, out_specs=tok(d),
        out_shape=jax.ShapeDtypeStruct((bsz, s, d), F32),
        compiler_params=_params("parallel", "parallel"), name="out_mlp",
    )(x, ys, yd, ym, mod, g2, wo, w1, w2, final_g)


def kernel(x, c, positions, norm1_g, norm2_g, w_ada, b_ada, w_in, w_out, ssm_a_re, ssm_a_im, ssm_log_dt, ssm_b_re, ssm_b_im, ssm_c_re, ssm_c_im, ssm_d, ssm_glu_w, ssm_glu_b, ssm_norm_g, diff_lq1, diff_lk1, diff_lq2, diff_lk2, diff_subln_g, moba_norm_g, mlp_w1, mlp_w2, final_g):
    bsz, s, d = x.shape
    depth = w_in.shape[0]
    w_ssm = ssm_d.shape[1]
    w_diff = DIFF_HEADS * 2 * HEAD_DIM
    w_moba = MOBA_HEADS * HEAD_DIM
    assert s % TOKEN_TILE == 0 and TOKEN_TILE % ATTN_BLOCK == 0 and s % (ATTN_CHUNK * ATTN_BLOCK) == 0
    assert s % SSM_TILE == 0 and SSM_TILE % (SSM_CHUNK * SUBLANES) == 0
    assert w_in.shape[2] == w_ssm + 3 * w_diff + 3 * w_moba

    cos_t, sin_t = _rope_tables(positions)
    mod_all = _ada_mod(c, w_ada, b_ada).reshape(depth, bsz, 6, d)
    sw1, sw2, sw3, scoef = _ssm_prep(ssm_a_re, ssm_a_im, ssm_log_dt, ssm_b_re, ssm_b_im, ssm_c_re, ssm_c_im)

    o_dq = w_ssm
    o_dk = o_dq + w_diff
    o_dv = o_dk + w_diff
    o_mq = o_dv + w_diff
    o_mk = o_mq + w_moba
    o_mv = o_mk + w_moba
    widths = (w_diff, w_diff, w_moba, w_moba, w_diff, w_moba)

    for l in range(depth):
        wl = w_in[l]
        wu = wl[:, :o_dq].astype(BF16)
        wt = jnp.concatenate([wl[:, o_dq:o_dk], wl[:, o_dk:o_dv], wl[:, o_mq:o_mk], wl[:, o_mk:o_mv],
                              wl[:, o_dv:o_mq], wl[:, o_mv:]], axis=1).T.astype(BF16)
        mod = mod_all[l]
        u, dq, dk, dv, mq, mk, mv = _projection(x, mod, norm1_g[l][None], wu, wt, cos_t, sin_t, widths)

        y_ssm = _ssm(u, sw1[l], sw2[l], sw3[l], scoef[l], ssm_d[l][None], ssm_glu_w[l].astype(BF16),
                     ssm_glu_b[l][None], ssm_norm_g[l][None])

        lam_init = 0.8 - 0.6 * math.exp(-0.3 * l)
        lam_params = jnp.stack([diff_lq1[l], diff_lk1[l], diff_lq2[l], diff_lk2[l]]).astype(F32)
        y_diff = _diff_attention(dq, dk, dv, lam_params, diff_subln_g[l], lam_init)
        y_moba = _moba_attention(mq, mk, mv, moba_norm_g[l])

        x = _out_mlp(x, y_ssm, y_diff, y_moba, mod, norm2_g[l][None], w_out[l].astype(BF16),
                     mlp_w1[l].astype(BF16), mlp_w2[l].astype(BF16), final_g[None],
                     final_norm=(l == depth - 1))
    return x
```

```python
import functools
import math

import jax
import jax.numpy as jnp
from jax import lax
from jax.experimental import pallas as pl
from jax.experimental.pallas import tpu as pltpu

F32 = jnp.float32
BF16 = jnp.bfloat16

SSM_GROUP = 16
SSM_STATE = 64
HEAD_DIM = 64
DIFF_HEADS = 4
MOBA_HEADS = 4
MOBA_BLOCK = 256
MOBA_TOPK = 3
ROPE_THETA = 500000.0
ROPE_HALF = 8
EPS = 1e-6
LOG2_E = math.log2(math.e)
NEG = -1e30
BELOW_NEG = -3e38

SUBLANES = 8
LANES = 128
TOKEN_TILE = 512
SSM_CHUNK = 4
SSM_TILE = 2048
ATTN_BLOCK = MOBA_BLOCK
ATTN_CHUNK = 4
VMEM_LIMIT = 56 * 1024 * 1024


def _params(*sem):
    return pltpu.CompilerParams(dimension_semantics=sem, vmem_limit_bytes=VMEM_LIMIT)


def _const_spec(shape):
    nd = len(shape)
    return pl.BlockSpec(shape, lambda *_: (0,) * nd)


def _rope_kernel(inv_ref, pos_ref, cos_ref, sin_ref):
    ang = inv_ref[...] * pos_ref[0].astype(F32)
    cos_ref[0] = jnp.cos(ang)
    sin_ref[0] = jnp.sin(ang)


def _rope_tables(positions):
    bsz, s = positions.shape
    rot = 2 * ROPE_HALF
    inv = (ROPE_THETA ** (-jnp.arange(0, rot, 2, dtype=F32) / rot)).reshape(ROPE_HALF, 1)
    out = jax.ShapeDtypeStruct((bsz, ROPE_HALF, s), F32)
    return pl.pallas_call(
        _rope_kernel,
        grid=(bsz,),
        in_specs=[_const_spec((ROPE_HALF, 1)),
                  pl.BlockSpec((1, 1, s), lambda b: (b, 0, 0))],
        out_specs=[pl.BlockSpec((1, ROPE_HALF, s), lambda b: (b, 0, 0))] * 2,
        out_shape=[out, out],
        compiler_params=_params("parallel"),
        name="rope_tables",
    )(inv, positions.reshape(bsz, 1, s))


def _ada_kernel(c_ref, w_ref, b_ref, o_ref):
    c = c_ref[...]
    sc = c * jax.nn.sigmoid(c)
    o_ref[0] = jnp.dot(sc, w_ref[0], preferred_element_type=F32,
                       precision=lax.Precision.HIGHEST) + b_ref[0]


def _ada_mod(c, w_ada, b_ada):
    depth, d, d6 = w_ada.shape
    bsz = c.shape[0]
    nchunk = d6 // d
    return pl.pallas_call(
        _ada_kernel,
        grid=(depth, nchunk),
        in_specs=[_const_spec((bsz, d)),
                  pl.BlockSpec((1, d, d), lambda l, j: (l, 0, j)),
                  pl.BlockSpec((1, 1, d), lambda l, j: (l, 0, j))],
        out_specs=pl.BlockSpec((1, bsz, d), lambda l, j: (l, 0, j)),
        out_shape=jax.ShapeDtypeStruct((depth, bsz, d6), F32),
        compiler_params=_params("parallel", "parallel"),
        name="ada_mod",
    )(c, w_ada, b_ada.reshape(depth, 1, d6))


def _ssm_prep_kernel(are_ref, aim_ref, ldt_ref, bre_ref, bim_ref,
                     bbr_ref, bbi_ref, p1r_ref, p1i_ref, pcr_ref, pci_ref):
    ar = are_ref[...]
    ai = aim_ref[...]
    dt = jnp.exp(ldt_ref[...])
    mag = jnp.exp(ar * dt)
    abr = mag * jnp.cos(ai * dt)
    abi = mag * jnp.sin(ai * dt)
    den = ar * ar + ai * ai
    nr = abr - 1.0
    ni = abi
    kr = (nr * ar + ni * ai) / den
    ki = (ni * ar - nr * ai) / den
    br = bre_ref[...]
    bi = bim_ref[...]
    bbr_ref[...] = kr[None] * br - ki[None] * bi
    bbi_ref[...] = kr[None] * bi + ki[None] * br
    pr, pi = abr, abi
    p1r_ref[0] = pr
    p1i_ref[0] = pi
    for j in range(1, SSM_CHUNK):
        pr, pi = pr * abr - pi * abi, pr * abi + pi * abr
        p1r_ref[j] = pr
        p1i_ref[j] = pi
    cr, ci = pr, pi
    pcr_ref[0] = cr
    pci_ref[0] = ci
    for j in range(1, SUBLANES):
        pr, pi = pr * cr - pi * ci, pr * ci + pi * cr
        pcr_ref[j] = pr
        pci_ref[j] = pi


def _ssm_prep(a_re, a_im, log_dt, b_re, b_im, c_re, c_im):
    depth, g, p = a_re.shape
    cch = b_re.shape[-1]
    gg = depth * g
    bt = lambda b: jnp.transpose(b, (3, 0, 1, 2)).reshape(cch, gg, p)
    shapes = ([jax.ShapeDtypeStruct((cch, gg, p), F32)] * 2 + [jax.ShapeDtypeStruct((SSM_CHUNK, gg, p), F32)] * 2
              + [jax.ShapeDtypeStruct((SUBLANES, gg, p), F32)] * 2)
    outs = pl.pallas_call(
        _ssm_prep_kernel, out_shape=shapes, name="ssm_prep",
    )(a_re.reshape(gg, p), a_im.reshape(gg, p), log_dt.reshape(gg, 1), bt(b_re), bt(b_im))
    per_layer = [x.reshape(x.shape[0], depth, g, p) for x in outs]
    a1, compact, coef = jax.vmap(_ssm_matrices, in_axes=(1, 1, 1, 1, 1, 1, 0, 0))(*per_layer, c_re, c_im)
    w1, w2, w3 = _ssm_expand(a1, compact, g, cch, p)
    return w1, w2, w3, coef


def _ssm_expand_kernel(a1_ref, cm_ref, w1_ref, w2_ref, w3_ref, *, groups, chans, states):
    nl = SSM_CHUNK
    lw = nl * groups * chans
    n = groups * states
    io = lambda shape, dim: lax.broadcasted_iota(jnp.int32, shape, dim)
    grp_tgc = lambda i: (i // chans) % groups
    grp_gp = lambda i: i // states
    col = io((nl * chans, lw), 1)
    e1 = jnp.where(io((nl * chans, lw), 0) == (col // (groups * chans)) * chans + col % chans, 1.0, 0.0).astype(BF16)
    e2 = jnp.where(io((states, n), 0) == io((states, n), 1) % states, 1.0, 0.0).astype(BF16)
    a1 = a1_ref[0].astype(BF16)
    m1 = grp_tgc(io((lw, lw), 0)) == grp_tgc(io((lw, lw), 1))
    w1_ref[0] = jnp.where(m1, jnp.dot(a1, e1, preferred_element_type=F32), 0.0).astype(BF16)
    m2 = grp_tgc(io((lw, n), 0)) == grp_gp(io((lw, n), 1))
    m3 = grp_gp(io((n, lw), 0)) == grp_tgc(io((n, lw), 1))
    for ri in range(2):
        x = jnp.dot(cm_ref[0, ri].astype(BF16), e2, preferred_element_type=F32)
        w2_ref[0, :, ri * n:(ri + 1) * n] = jnp.where(m2, x, 0.0).astype(BF16)
        y = lax.dot_general(e2, cm_ref[0, 2 + ri].astype(BF16), (((0,), (1,)), ((), ())),
                            preferred_element_type=F32)
        w3_ref[0, ri * n:(ri + 1) * n, :] = jnp.where(m3, y, 0.0).astype(BF16)


def _ssm_expand(a1, compact, groups, chans, states):
    depth = a1.shape[0]
    lw = SSM_CHUNK * groups * chans
    n = groups * states
    kern = functools.partial(_ssm_expand_kernel, groups=groups, chans=chans, states=states)
    return pl.pallas_call(
        kern, grid=(depth,),
        in_specs=[pl.BlockSpec((1,) + a1.shape[1:], lambda l: (l, 0, 0)),
                  pl.BlockSpec((1,) + compact.shape[1:], lambda l: (l, 0, 0, 0))],
        out_specs=[pl.BlockSpec((1, lw, lw), lambda l: (l, 0, 0)),
                   pl.BlockSpec((1, lw, 2 * n), lambda l: (l, 0, 0)),
                   pl.BlockSpec((1, 2 * n, lw), lambda l: (l, 0, 0))],
        out_shape=[jax.ShapeDtypeStruct((depth, lw, lw), BF16), jax.ShapeDtypeStruct((depth, lw, 2 * n), BF16),
                   jax.ShapeDtypeStruct((depth, 2 * n, lw), BF16)],
        compiler_params=_params("parallel"), name="ssm_expand",
    )(a1, compact)


def _ssm_matrices(bbr, bbi, p1r, p1i, pcr, pci, c_re, c_im):
    cch, g, p = bbr.shape
    w = g * cch
    n = g * p
    nl = SSM_CHUNK
    hi = lax.Precision.HIGHEST
    ar = jnp.concatenate([jnp.ones((1, g, p), F32), p1r], axis=0)
    ai = jnp.concatenate([jnp.zeros((1, g, p), F32), p1i], axis=0)
    car = c_re[None] * ar[:, :, None, :] - c_im[None] * ai[:, :, None, :]
    cai = c_re[None] * ai[:, :, None, :] + c_im[None] * ar[:, :, None, :]
    kk = (jnp.einsum('tgcp,dgp->tgcd', car[:nl], bbr, precision=hi)
          - jnp.einsum('tgcp,dgp->tgcd', cai[:nl], bbi, precision=hi))
    zk = jnp.zeros_like(kk[0])
    ksel = jnp.stack([jnp.stack([kk[t2 - t] if t2 >= t else zk for t2 in range(nl)])
                      for t in range(nl)])
    a1 = jnp.transpose(ksel, (0, 2, 4, 1, 3)).reshape(nl * w, nl * cch)
    rev_r = jnp.stack([ar[nl - 1 - t] for t in range(nl)])
    rev_i = jnp.stack([ai[nl - 1 - t] for t in range(nl)])
    abr_ = rev_r[:, None] * bbr[None] - rev_i[:, None] * bbi[None]
    abi_ = rev_r[:, None] * bbi[None] + rev_i[:, None] * bbr[None]
    rows2 = lambda x: jnp.transpose(x, (0, 2, 1, 3)).reshape(nl * w, p)
    compact = jnp.stack([rows2(abr_), rows2(abi_),
                         car[1:].reshape(nl * w, p), -cai[1:].reshape(nl * w, p)])
    pcr = pcr.reshape(SUBLANES, n)
    pci = pci.reshape(SUBLANES, n)
    row = jnp.arange(SUBLANES)[:, None]
    tabs = []
    for shift in (1, 2, 4):
        keep = row >= shift
        tabs.append(jnp.where(keep, pcr[shift - 1][None], 0.0))
        tabs.append(jnp.where(keep, pci[shift - 1][None], 0.0))
    tabs += [pcr, pci]
    return a1, compact, jnp.stack(tabs)


def _rmsnorm_rows(x, g):
    return x * lax.rsqrt(jnp.mean(x * x, axis=-1, keepdims=True) + EPS) * g


def _proj_kernel(x_ref, mod_ref, g_ref, wu_ref, wt_ref, cos_ref, sin_ref,
                 u_ref, dq_ref, dk_ref, dv_ref, mq_ref, mk_ref, mv_ref, *, n_rot, widths):
    wdq, wdk, wmq, wmk, wdv, wmv = widths
    x = x_ref[0]
    mod = mod_ref[0]
    h = _rmsnorm_rows(x, g_ref[...]) * (1.0 + mod[1:2]) + mod[0:1]
    hb = h.astype(BF16)
    u = jnp.dot(hb, wu_ref[...], preferred_element_type=F32)
    for j in range(u_ref.shape[1]):
        u_ref[0, j] = u[:, j * LANES:(j + 1) * LANES]
    pt = lax.dot_general(wt_ref[...], hb, (((1,), (1,)), ((), ())), preferred_element_type=F32)
    tm = pt.shape[1]
    rot = pt[:n_rot].reshape(n_rot // HEAD_DIM, HEAD_DIM, tm)
    cos = cos_ref[0][None]
    sin = sin_ref[0][None]
    x1 = rot[:, 0:ROPE_HALF]
    x2 = rot[:, ROPE_HALF:2 * ROPE_HALF]
    rot = jnp.concatenate([x1 * cos - x2 * sin, x2 * cos + x1 * sin, rot[:, 2 * ROPE_HALF:]], axis=1)
    rot = rot.reshape(n_rot, tm)
    qscale = HEAD_DIM ** -0.5 * LOG2_E
    o = 0
    dq = (rot[o:o + wdq] * qscale).astype(BF16); o += wdq
    dk_ref[0] = rot[o:o + wdk].T.astype(BF16); o += wdk
    mq = (rot[o:o + wmq] * qscale).astype(BF16); o += wmq
    mk_ref[0] = rot[o:o + wmk].T.astype(BF16); o += wmk
    dv = pt[o:o + wdv].astype(BF16); o += wdv
    mv = pt[o:o + wmv].astype(BF16)
    for j in range(tm // ATTN_BLOCK):
        blk = slice(j * ATTN_BLOCK, (j + 1) * ATTN_BLOCK)
        dq_ref[0, j] = dq[:, blk]
        dv_ref[0, j] = dv[:, blk]
        mq_ref[0, j] = mq[:, blk]
        mv_ref[0, j] = mv[:, blk]


def _projection(x, mod, g, wu, wt, cos_t, sin_t, widths):
    bsz, s, d = x.shape
    tm = TOKEN_TILE
    ns = s // tm
    wdq, wdk, wmq, wmk, wdv, wmv = widths
    n_rot = wdq + wdk + wmq + wmk
    ab = ATTN_BLOCK
    nab = tm // ab
    kern = functools.partial(_proj_kernel, n_rot=n_rot, widths=widths)
    tiles_t = lambda w: jax.ShapeDtypeStruct((bsz, s // ab, w, ab), BF16)
    out_shape = [
        jax.ShapeDtypeStruct((bsz, wu.shape[1] // LANES, s, LANES), F32),
        tiles_t(wdq),
        jax.ShapeDtypeStruct((bsz, s, wdk), BF16),
        tiles_t(wdv),
        tiles_t(wmq),
        jax.ShapeDtypeStruct((bsz, s, wmk), BF16),
        tiles_t(wmv),
    ]
    spec_t = lambda w: pl.BlockSpec((1, nab, w, ab), lambda b, i: (b, i, 0, 0))
    out_specs = [
        pl.BlockSpec((1, wu.shape[1] // LANES, tm, LANES), lambda b, i: (b, 0, i, 0)),
        spec_t(wdq),
        pl.BlockSpec((1, tm, wdk), lambda b, i: (b, i, 0)),
        spec_t(wdv),
        spec_t(wmq),
        pl.BlockSpec((1, tm, wmk), lambda b, i: (b, i, 0)),
        spec_t(wmv),
    ]
    in_specs = [
        pl.BlockSpec((1, tm, d), lambda b, i: (b, i, 0)),
        pl.BlockSpec((1, 6, d), lambda b, i: (b, 0, 0)),
        _const_spec((1, d)),
        _const_spec(wu.shape),
        _const_spec(wt.shape),
        pl.BlockSpec((1, ROPE_HALF, tm), lambda b, i: (b, 0, i)),
        pl.BlockSpec((1, ROPE_HALF, tm), lambda b, i: (b, 0, i)),
    ]
    return pl.pallas_call(
        kern, grid=(bsz, ns), in_specs=in_specs, out_specs=out_specs, out_shape=out_shape,
        compiler_params=_params("parallel", "parallel"), name="projection",
    )(x, mod, g, wu, wt, cos_t, sin_t)


def _gelu_tanh(y):
    return 0.5 * y * (1.0 + jnp.tanh(math.sqrt(2.0 / math.pi) * (y + 0.044715 * (y * y * y))))


def _ssm_kernel(u_ref, w1_ref, w2_ref, w3_ref, coef_ref, d_ref, gw_ref, gb_ref, ng_ref,
                y_ref, st_ref, yt_ref, carry_ref, *, n_state):
    n = n_state
    nl = SSM_CHUNK

    @pl.when(pl.program_id(1) == 0)
    def _():
        carry_ref[...] = jnp.zeros_like(carry_ref)

    ns, tm = u_ref.shape[1], u_ref.shape[2]
    w = ns * LANES
    mc = tm // nl
    uc = jnp.concatenate([u_ref[0, j, pl.ds(t, mc, stride=nl), :] for t in range(nl) for j in range(ns)],
                         axis=1).astype(BF16)
    st_ref[...] = jnp.dot(uc, w2_ref[...], preferred_element_type=F32)

    def cmul_add(xr, xi, ar, ai, sr, si):
        return xr + (ar * sr - ai * si), xi + (ar * si + ai * sr)

    first_row = lax.broadcasted_iota(jnp.int32, (SUBLANES, n), 0) == 0

    def group(r, carry):
        cr, ci = carry
        rows = pl.ds(pl.multiple_of(r * SUBLANES, SUBLANES), SUBLANES)
        xr = st_ref[rows, 0:n]
        xi = st_ref[rows, n:2 * n]
        for lvl, shift in enumerate((1, 2, 4)):
            sr = pltpu.roll(xr, shift, axis=0)
            si = pltpu.roll(xi, shift, axis=0)
            xr, xi = cmul_add(xr, xi, coef_ref[2 * lvl], coef_ref[2 * lvl + 1], sr, si)
        xr, xi = cmul_add(xr, xi, coef_ref[6], coef_ref[7], cr, ci)
        st_ref[rows, 0:n] = jnp.where(first_row, cr, pltpu.roll(xr, 1, axis=0))
        st_ref[rows, n:2 * n] = jnp.where(first_row, ci, pltpu.roll(xi, 1, axis=0))
        last = SUBLANES - 1
        return (jnp.broadcast_to(xr[last:last + 1], xr.shape),
                jnp.broadcast_to(xi[last:last + 1], xi.shape))

    cr, ci = lax.fori_loop(0, mc // SUBLANES, group, (carry_ref[0], carry_ref[1]))
    carry_ref[0] = cr
    carry_ref[1] = ci

    yc = (jnp.dot(uc, w1_ref[...], preferred_element_type=F32)
          + jnp.dot(st_ref[...].astype(BF16), w3_ref[...], preferred_element_type=F32))
    for t in range(nl):
        for j in range(ns):
            yt_ref[j, pl.ds(t, mc, stride=nl), :] = yc[:, t * w + j * LANES:t * w + (j + 1) * LANES]
    u = jnp.concatenate([u_ref[0, j] for j in range(ns)], axis=1)
    y = jnp.concatenate([yt_ref[j] for j in range(ns)], axis=1) + d_ref[...] * u
    y = _gelu_tanh(y)
    z = jnp.dot(y.astype(BF16), gw_ref[...], preferred_element_type=F32) + gb_ref[...]
    y = y * jax.nn.sigmoid(z)
    y_ref[0] = _rmsnorm_rows(y, ng_ref[...]).astype(y_ref.dtype)


def _ssm(u, w1, w2, w3, coef, d_skip, glu_w, glu_b, norm_g):
    bsz, ns, s, _ = u.shape
    w = ns * LANES
    n = coef.shape[-1]
    tm = SSM_TILE
    kern = functools.partial(_ssm_kernel, n_state=n)
    in_specs = [
        pl.BlockSpec((1, ns, tm, LANES), lambda b, i: (b, 0, i, 0)),
        _const_spec(w1.shape), _const_spec(w2.shape), _const_spec(w3.shape), _const_spec(coef.shape),
        _const_spec((1, w)), _const_spec((w, w)), _const_spec((1, w)), _const_spec((1, w)),
    ]
    return pl.pallas_call(
        kern, grid=(bsz, s // tm), in_specs=in_specs,
        out_specs=pl.BlockSpec((1, tm, w), lambda b, i: (b, i, 0)),
        out_shape=jax.ShapeDtypeStruct((bsz, s, w), BF16),
        scratch_shapes=[pltpu.VMEM((tm // SSM_CHUNK, 2 * n), F32), pltpu.VMEM((ns, tm, LANES), F32),
                        pltpu.VMEM((2, SUBLANES, n), F32)],
        compiler_params=_params("parallel", "arbitrary"), name="ssm",
    )(u, w1, w2, w3, coef, d_skip, glu_w, glu_b, norm_g)


def _row_groups(x):
    rows, lanes = x.shape
    return x.reshape(4, rows // (4 * SUBLANES), SUBLANES, lanes)


def _score_stage(k_ref, first_blk, n_gran, qz, tile_bias, s_ref, mx_ref):
    ab = ATTN_BLOCK
    for w in range(n_gran):
        kb = k_ref[0, pl.ds(pl.multiple_of((first_blk + w) * ab, ab), ab), :]
        s = jnp.dot(kb, qz, preferred_element_type=F32)
        tb = tile_bias(w)
        if tb is not None:
            s = s + tb
        s_ref[w * ab:(w + 1) * ab, :] = s
        mx_ref[w] = jnp.max(jnp.max(_row_groups(s), axis=1), axis=0)


def _softmax_stage(s_ref, mx_ref, vt_ref, first_blk, n_gran, v_rows, row_bias, m, l, acc_ref):
    ab = ATTN_BLOCK
    rbs = [row_bias(w) for w in range(n_gran)]
    raw = mx_ref[0]
    seen = raw if rbs[0] is None else raw + rbs[0]
    for w in range(1, n_gran):
        raw = jnp.maximum(raw, mx_ref[w])
        seen = jnp.maximum(seen, mx_ref[w] if rbs[w] is None else mx_ref[w] + rbs[w])
    m_new = jnp.maximum(m, jnp.max(seen, axis=0, keepdims=True))
    shift = jnp.where(m_new > 0.5 * NEG, m_new, jnp.max(raw, axis=0, keepdims=True))
    alpha = jnp.exp2(m - m_new)
    psum = None
    pv = None
    for w in range(n_gran):
        p = jnp.exp2(s_ref[w * ab:(w + 1) * ab, :] - (shift if rbs[w] is None else shift - rbs[w]))
        part = jnp.sum(jnp.sum(_row_groups(p), axis=1), axis=0)
        psum = part if psum is None else psum + part
        d = jnp.dot(vt_ref[0, first_blk + w, v_rows, :], p.astype(BF16), preferred_element_type=F32)
        pv = d if pv is None else pv + d
    acc_ref[...] = alpha * acc_ref[...] + pv
    return m_new, alpha * l + jnp.sum(psum, axis=0, keepdims=True)


def _flash_tile(g, k_ref, vt_ref, chains, full_bias, diag_bias, qz_ref, s_ref, mx_ref, acc_ref):
    nc = len(chains)
    t = s_ref.shape[-1]
    acc_ref[...] = jnp.zeros(acc_ref.shape, F32)
    neg = jnp.full((1, t), NEG, F32)
    zero = jnp.zeros((1, t), F32)

    def score(c, first, n_gran, bias):
        _score_stage(k_ref, first, n_gran, qz_ref[c], bias[0], s_ref.at[c], mx_ref.at[c])

    def softmax(c, first, n_gran, bias, m, l):
        return _softmax_stage(s_ref.at[c], mx_ref.at[c], vt_ref, first, n_gran, chains[c][1], bias[1],
                              m, l, acc_ref.at[c])

    score(0, 0, ATTN_CHUNK, full_bias(0, 0))

    def step(i, carry):
        first = i * ATTN_CHUNK
        out = []
        for c in range(nc):
            if c + 1 < nc:
                score(c + 1, first, ATTN_CHUNK, full_bias(c + 1, first))
            else:
                score(0, first + ATTN_CHUNK, ATTN_CHUNK, full_bias(0, first + ATTN_CHUNK))
            out += softmax(c, first, ATTN_CHUNK, full_bias(c, first), carry[2 * c], carry[2 * c + 1])
        return tuple(out)

    carry = lax.fori_loop(0, g, step, (neg, zero) * nc)
    first = g * ATTN_CHUNK
    out = []
    score(0, first, chains[0][0] + 1, diag_bias(0))
    for c in range(nc):
        if c + 1 < nc:
            score(c + 1, first, chains[c + 1][0] + 1, diag_bias(c + 1))
        out.append(softmax(c, first, chains[c][0] + 1, diag_bias(c), carry[2 * c], carry[2 * c + 1]))
    return out


def _flash_scratch(nc, dv):
    ab = ATTN_BLOCK
    return [pltpu.VMEM((nc, 2 * HEAD_DIM, ab), BF16),
            pltpu.VMEM((nc, ATTN_CHUNK * ab, ab), F32),
            pltpu.VMEM((nc, ATTN_CHUNK, SUBLANES, ab), F32),
            pltpu.VMEM((nc, dv, ab), F32),
            pltpu.VMEM((ab, ab), F32)]


def _causal_bias(t):
    key = lax.broadcasted_iota(jnp.int32, (t, t), 0)
    qry = lax.broadcasted_iota(jnp.int32, (t, t), 1)
    return jnp.where(key <= qry, 0.0, NEG)


def _diff_kernel(lam_ref, g_ref, qt_ref, k_ref, vt_ref, o_ref, qz_ref, s_ref, mx_ref, acc_ref, tri_ref,
                 *, lam_init):
    g = pl.program_id(2)
    ab = ATTN_BLOCK
    chains = [(qb, slice(None)) for qb in range(ATTN_CHUNK) for _ in range(2)]
    tri_ref[...] = _causal_bias(ab)
    row = lax.broadcasted_iota(jnp.int32, (2 * HEAD_DIM, ab), 0)
    for qb in range(ATTN_CHUNK):
        qt = qt_ref[0, qb].astype(F32)
        qz_ref[2 * qb] = jnp.where(row < HEAD_DIM, qt, 0.0).astype(BF16)
        qz_ref[2 * qb + 1] = jnp.where(row >= HEAD_DIM, qt, 0.0).astype(BF16)

    none = lambda w: None

    def diag_bias(c):
        qb = chains[c][0]
        return (lambda w: tri_ref[...] if w == qb else None), none

    ml = _flash_tile(g, k_ref, vt_ref, chains, lambda c, first: (none, none), diag_bias,
                     qz_ref, s_ref, mx_ref, acc_ref)

    lp = lam_ref[...]
    lam = (jnp.exp(jnp.sum(lp[0:1] * lp[1:2], axis=1, keepdims=True))
           - jnp.exp(jnp.sum(lp[2:3] * lp[3:4], axis=1, keepdims=True)) + lam_init)
    for qb in range(ATTN_CHUNK):
        o = acc_ref[2 * qb] / ml[2 * qb][1] - lam * (acc_ref[2 * qb + 1] / ml[2 * qb + 1][1])
        o = o * lax.rsqrt(jnp.mean(o * o, axis=0, keepdims=True) + EPS) * g_ref[...]
        o_ref[0, qb * ab:(qb + 1) * ab, :] = (o * (1.0 - lam_init)).T.astype(o_ref.dtype)


def _diff_attention(qt, k, vt, lam_params, subln_g, lam_init):
    bsz, nb, wq, ab = qt.shape
    s = k.shape[1]
    dv = 2 * HEAD_DIM
    nh = wq // dv
    tq = ATTN_CHUNK * ab
    kern = functools.partial(_diff_kernel, lam_init=lam_init)
    in_specs = [
        _const_spec(lam_params.shape),
        _const_spec((dv, 1)),
        pl.BlockSpec((1, ATTN_CHUNK, dv, ab), lambda b, h, i: (b, i, h, 0)),
        pl.BlockSpec((1, s, dv), lambda b, h, i: (b, 0, h)),
        pl.BlockSpec((1, nb, dv, ab), lambda b, h, i: (b, 0, h, 0)),
    ]
    return pl.pallas_call(
        kern, grid=(bsz, nh, nb // ATTN_CHUNK), in_specs=in_specs,
        out_specs=pl.BlockSpec((1, tq, dv), lambda b, h, i: (b, i, h)),
        out_shape=jax.ShapeDtypeStruct((bsz, s, wq), BF16),
        scratch_shapes=_flash_scratch(2 * ATTN_CHUNK, dv),
        compiler_params=_params("parallel", "parallel", "arbitrary"), name="diff_attention",
    )(lam_params, subln_g.reshape(dv, 1), qt, k, vt)


def _moba_kernel(g_ref, qt_ref, k_ref, vt_ref, o_ref, kmean_ref, bias_ref, qz_ref, s_ref, mx_ref, acc_ref,
                 tri_ref, *, n_blocks):
    g = pl.program_id(2)
    ab = ATTN_BLOCK
    tq = ATTN_CHUNK * ab
    pair = 2 * HEAD_DIM

    @pl.when(g == 0)
    def _():
        def mean_block(j, c):
            kb = k_ref[0, pl.ds(pl.multiple_of(j * ab, ab), ab), :].astype(F32)
            kmean_ref[pl.ds(j, 1), :] = jnp.mean(kb, axis=0, keepdims=True)
            return c
        lax.fori_loop(0, n_blocks, mean_block, 0)

    tri_ref[...] = _causal_bias(ab)
    chains = [(qb, slice(hh * HEAD_DIM, (hh + 1) * HEAD_DIM)) for qb in range(ATTN_CHUNK) for hh in range(2)]
    qt = jnp.concatenate([qt_ref[0, qb] for qb in range(ATTN_CHUNK)], axis=1).astype(F32)
    row = lax.broadcasted_iota(jnp.int32, (pair, tq), 0)
    lane = lax.broadcasted_iota(jnp.int32, (n_blocks, pair), 1)
    bid = lax.broadcasted_iota(jnp.int32, (n_blocks, tq), 0)
    qblk = g * ATTN_CHUNK + lax.broadcasted_iota(jnp.int32, (n_blocks, tq), 1) // ab
    past = bid < qblk
    bid = bid.astype(F32)
    kmean = kmean_ref[...]
    for hh in range(2):
        qf = jnp.where((row >= hh * HEAD_DIM) & (row < (hh + 1) * HEAD_DIM), qt, 0.0)
        for qb in range(ATTN_CHUNK):
            qz_ref[2 * qb + hh] = qf[:, qb * ab:(qb + 1) * ab].astype(BF16)
        km = jnp.where((lane >= hh * HEAD_DIM) & (lane < (hh + 1) * HEAD_DIM), kmean, 0.0)
        gate = jnp.dot(km, qf, preferred_element_type=F32, precision=lax.Precision.HIGHEST)
        gate = jnp.where(past, gate, NEG)
        sel = jnp.zeros_like(gate)
        for _ in range(MOBA_TOPK):
            top = jnp.max(gate, axis=0, keepdims=True)
            first = jnp.min(jnp.where(gate == top, bid, float(n_blocks)), axis=0, keepdims=True)
            hit = bid == first
            sel = jnp.where(hit, 1.0, sel)
            gate = jnp.where(hit, BELOW_NEG, gate)
        bias_ref[hh] = jnp.where(past, jnp.where(sel > 0.0, 0.0, NEG), NEG)

    none = lambda w: None

    def picked(c, blk):
        qb, hh = c // 2, c % 2
        return bias_ref[hh, pl.ds(blk, 1), qb * ab:(qb + 1) * ab]

    def full_bias(c, first):
        return none, lambda w: picked(c, first + w)

    def diag_bias(c):
        qb = chains[c][0]
        return ((lambda w: tri_ref[...] if w == qb else None),
                (lambda w: None if w == qb else picked(c, g * ATTN_CHUNK + w)))

    ml = _flash_tile(g, k_ref, vt_ref, chains, full_bias, diag_bias, qz_ref, s_ref, mx_ref, acc_ref)

    for qb in range(ATTN_CHUNK):
        outs = []
        for hh in range(2):
            o = acc_ref[2 * qb + hh] / ml[2 * qb + hh][1]
            outs.append(o * lax.rsqrt(jnp.mean(o * o, axis=0, keepdims=True) + EPS) * g_ref[...])
        o_ref[0, qb * ab:(qb + 1) * ab, :] = jnp.concatenate(outs, axis=0).T.astype(o_ref.dtype)


def _moba_attention(qt, k, vt, norm_g):
    bsz, nb, wq, ab = qt.shape
    s = k.shape[1]
    pair = 2 * HEAD_DIM
    tq = ATTN_CHUNK * ab
    kern = functools.partial(_moba_kernel, n_blocks=nb)
    in_specs = [
        _const_spec((HEAD_DIM, 1)),
        pl.BlockSpec((1, ATTN_CHUNK, pair, ab), lambda b, h, i: (b, i, h, 0)),
        pl.BlockSpec((1, s, pair), lambda b, h, i: (b, 0, h)),
        pl.BlockSpec((1, nb, pair, ab), lambda b, h, i: (b, 0, h, 0)),
    ]
    return pl.pallas_call(
        kern, grid=(bsz, wq // pair, nb // ATTN_CHUNK), in_specs=in_specs,
        out_specs=pl.BlockSpec((1, tq, pair), lambda b, h, i: (b, i, h)),
        out_shape=jax.ShapeDtypeStruct((bsz, s, wq), BF16),
        scratch_shapes=[pltpu.VMEM((nb, pair), F32), pltpu.VMEM((2, nb, tq), F32)]
        + _flash_scratch(2 * ATTN_CHUNK, HEAD_DIM),
        compiler_params=_params("parallel", "parallel", "arbitrary"), name="moba_attention",
    )(norm_g.reshape(HEAD_DIM, 1), qt, k, vt)


def _out_mlp_kernel(x_ref, ys_ref, yd_ref, ym_ref, mod_ref, g2_ref, wo_ref, w1_ref, w2_ref, fg_ref,
                    o_ref, *, final_norm, ff_chunk):
    x = x_ref[0]
    mod = mod_ref[0]
    ws = ys_ref.shape[2]
    wd = yd_ref.shape[2]
    mix = (jnp.dot(ys_ref[0], wo_ref[0:ws], preferred_element_type=F32)
           + jnp.dot(yd_ref[0], wo_ref[ws:ws + wd], preferred_element_type=F32)
           + jnp.dot(ym_ref[0], wo_ref[ws + wd:], preferred_element_type=F32))
    x = x + mod[2:3] * mix
    h = (_rmsnorm_rows(x, g2_ref[...]) * (1.0 + mod[4:5]) + mod[3:4]).astype(BF16)
    acc = jnp.zeros_like(x)
    for c in range(w1_ref.shape[1] // ff_chunk):
        a = jnp.dot(h, w1_ref[:, c * ff_chunk:(c + 1) * ff_chunk], preferred_element_type=F32)
        a = jnp.square(jnp.maximum(a, 0.0)).astype(BF16)
        acc = acc + jnp.dot(a, w2_ref[c * ff_chunk:(c + 1) * ff_chunk, :], preferred_element_type=F32)
    x = x + mod[5:6] * acc
    if final_norm:
        x = _rmsnorm_rows(x, fg_ref[...])
    o_ref[0] = x


def _out_mlp(x, ys, yd, ym, mod, g2, wo, w1, w2, final_g, final_norm):
    bsz, s, d = x.shape
    tm = TOKEN_TILE
    kern = functools.partial(_out_mlp_kernel, final_norm=final_norm, ff_chunk=d)
    tok = lambda w: pl.BlockSpec((1, tm, w), lambda b, i: (b, i, 0))
    in_specs = [
        tok(d), tok(ys.shape[2]), tok(yd.shape[2]), tok(ym.shape[2]),
        pl.BlockSpec((1, 6, d), lambda b, i: (b, 0, 0)),
        _const_spec((1, d)), _const_spec(wo.shape), _const_spec(w1.shape), _const_spec(w2.shape),
        _const_spec((1, d)),
    ]
    return pl.pallas_call(
        kern, grid=(bsz, s // tm), in_specs=in_specs, out_specs=tok(d),
        out_shape=jax.ShapeDtypeStruct((bsz, s, d), F32),
        compiler_params=_params("parallel", "parallel"), name="out_mlp",
    )(x, ys, yd, ym, mod, g2, wo, w1, w2, final_g)


def kernel(x, c, positions, norm1_g, norm2_g, w_ada, b_ada, w_in, w_out, ssm_a_re, ssm_a_im, ssm_log_dt, ssm_b_re, ssm_b_im, ssm_c_re, ssm_c_im, ssm_d, ssm_glu_w, ssm_glu_b, ssm_norm_g, diff_lq1, diff_lk1, diff_lq2, diff_lk2, diff_subln_g, moba_norm_g, mlp_w1, mlp_w2, final_g):
    bsz, s, d = x.shape
    depth = w_in.shape[0]
    w_ssm = ssm_d.shape[1]
    w_diff = DIFF_HEADS * 2 * HEAD_DIM
    w_moba = MOBA_HEADS * HEAD_DIM
    assert s % TOKEN_TILE == 0 and TOKEN_TILE % ATTN_BLOCK == 0 and s % (ATTN_CHUNK * ATTN_BLOCK) == 0
    assert s % SSM_TILE == 0 and SSM_TILE % (SSM_CHUNK * SUBLANES) == 0
    assert w_in.shape[2] == w_ssm + 3 * w_diff + 3 * w_moba

    cos_t, sin_t = _rope_tables(positions)
    mod_all = _ada_mod(c, w_ada, b_ada).reshape(depth, bsz, 6, d)
    sw1, sw2, sw3, scoef = _ssm_prep(ssm_a_re, ssm_a_im, ssm_log_dt, ssm_b_re, ssm_b_im, ssm_c_re, ssm_c_im)

    o_dq = w_ssm
    o_dk = o_dq + w_diff
    o_dv = o_dk + w_diff
    o_mq = o_dv + w_diff
    o_mk = o_mq + w_moba
    o_mv = o_mk + w_moba
    widths = (w_diff, w_diff, w_moba, w_moba, w_diff, w_moba)

    for l in range(depth):
        wl = w_in[l]
        wu = wl[:, :o_dq].astype(BF16)
        wt = jnp.concatenate([wl[:, o_dq:o_dk], wl[:, o_dk:o_dv], wl[:, o_mq:o_mk], wl[:, o_mk:o_mv],
                              wl[:, o_dv:o_mq], wl[:, o_mv:]], axis=1).T.astype(BF16)
        mod = mod_all[l]
        u, dq, dk, dv, mq, mk, mv = _projection(x, mod, norm1_g[l][None], wu, wt, cos_t, sin_t, widths)

        y_ssm = _ssm(u, sw1[l], sw2[l], sw3[l], scoef[l], ssm_d[l][None], ssm_glu_w[l].astype(BF16),
                     ssm_glu_b[l][None], ssm_norm_g[l][None])

        lam_init = 0.8 - 0.6 * math.exp(-0.3 * l)
        lam_params = jnp.stack([diff_lq1[l], diff_lk1[l], diff_lq2[l], diff_lk2[l]]).astype(F32)
        y_diff = _diff_attention(dq, dk, dv, lam_params, diff_subln_g[l], lam_init)
        y_moba = _moba_attention(mq, mk, mv, moba_norm_g[l])

        x = _out_mlp(x, y_ssm, y_diff, y_moba, mod, norm2_g[l][None], w_out[l].astype(BF16),
                     mlp_w1[l].astype(BF16), mlp_w2[l].astype(BF16), final_g[None],
                     final_norm=(l == depth - 1))
    return x
```

```python
import functools
import math

import jax
import jax.numpy as jnp
from jax import lax
from jax.experimental import pallas as pl
from jax.experimental.pallas import tpu as pltpu

F32 = jnp.float32
BF16 = jnp.bfloat16

SSM_GROUP = 16
SSM_STATE = 64
HEAD_DIM = 64
DIFF_HEADS = 4
MOBA_HEADS = 4
MOBA_BLOCK = 256
MOBA_TOPK = 3
ROPE_THETA = 500000.0
ROPE_HALF = 8
EPS = 1e-6
LOG2_E = math.log2(math.e)
NEG = -1e30
BELOW_NEG = -3e38

SUBLANES = 8
LANES = 128
TOKEN_TILE = 512
SSM_CHUNK = 4
SSM_TILE = 2048
ATTN_BLOCK = MOBA_BLOCK
ATTN_CHUNK = 4
VMEM_LIMIT = 56 * 1024 * 1024


def _params(*sem):
    return pltpu.CompilerParams(dimension_semantics=sem, vmem_limit_bytes=VMEM_LIMIT)


def _const_spec(shape):
    nd = len(shape)
    return pl.BlockSpec(shape, lambda *_: (0,) * nd)


def _rope_kernel(inv_ref, pos_ref, cos_ref, sin_ref):
    ang = inv_ref[...] * pos_ref[0].astype(F32)
    cos_ref[0] = jnp.cos(ang)
    sin_ref[0] = jnp.sin(ang)


def _rope_tables(positions):
    bsz, s = positions.shape
    rot = 2 * ROPE_HALF
    inv = (ROPE_THETA ** (-jnp.arange(0, rot, 2, dtype=F32) / rot)).reshape(ROPE_HALF, 1)
    out = jax.ShapeDtypeStruct((bsz, ROPE_HALF, s), F32)
    return pl.pallas_call(
        _rope_kernel,
        grid=(bsz,),
        in_specs=[_const_spec((ROPE_HALF, 1)),
                  pl.BlockSpec((1, 1, s), lambda b: (b, 0, 0))],
        out_specs=[pl.BlockSpec((1, ROPE_HALF, s), lambda b: (b, 0, 0))] * 2,
        out_shape=[out, out],
        compiler_params=_params("parallel"),
        name="rope_tables",
    )(inv, positions.reshape(bsz, 1, s))


def _ada_kernel(c_ref, w_ref, b_ref, o_ref):
    c = c_ref[...]
    sc = c * jax.nn.sigmoid(c)
    o_ref[0] = jnp.dot(sc, w_ref[0], preferred_element_type=F32,
                       precision=lax.Precision.HIGHEST) + b_ref[0]


def _ada_mod(c, w_ada, b_ada):
    depth, d, d6 = w_ada.shape
    bsz = c.shape[0]
    nchunk = d6 // d
    return pl.pallas_call(
        _ada_kernel,
        grid=(depth, nchunk),
        in_specs=[_const_spec((bsz, d)),
                  pl.BlockSpec((1, d, d), lambda l, j: (l, 0, j)),
                  pl.BlockSpec((1, 1, d), lambda l, j: (l, 0, j))],
        out_specs=pl.BlockSpec((1, bsz, d), lambda l, j: (l, 0, j)),
        out_shape=jax.ShapeDtypeStruct((depth, bsz, d6), F32),
        compiler_params=_params("parallel", "parallel"),
        name="ada_mod",
    )(c, w_ada, b_ada.reshape(depth, 1, d6))


def _ssm_prep_kernel(are_ref, aim_ref, ldt_ref, bre_ref, bim_ref,
                     bbr_ref, bbi_ref, p1r_ref, p1i_ref, pcr_ref, pci_ref):
    ar = are_ref[...]
    ai = aim_ref[...]
    dt = jnp.exp(ldt_ref[...])
    mag = jnp.exp(ar * dt)
    abr = mag * jnp.cos(ai * dt)
    abi = mag * jnp.sin(ai * dt)
    den = ar * ar + ai * ai
    nr = abr - 1.0
    ni = abi
    kr = (nr * ar + ni * ai) / den
    ki = (ni * ar - nr * ai) / den
    br = bre_ref[...]
    bi = bim_ref[...]
    bbr_ref[...] = kr[None] * br - ki[None] * bi
    bbi_ref[...] = kr[None] * bi + ki[None] * br
    pr, pi = abr, abi
    p1r_ref[0] = pr
    p1i_ref[0] = pi
    for j in range(1, SSM_CHUNK):
        pr, pi = pr * abr - pi * abi, pr * abi + pi * abr
        p1r_ref[j] = pr
        p1i_ref[j] = pi
    cr, ci = pr, pi
    pcr_ref[0] = cr
    pci_ref[0] = ci
    for j in range(1, SUBLANES):
        pr, pi = pr * cr - pi * ci, pr * ci + pi * cr
        pcr_ref[j] = pr
        pci_ref[j] = pi


def _ssm_prep(a_re, a_im, log_dt, b_re, b_im, c_re, c_im):
    depth, g, p = a_re.shape
    cch = b_re.shape[-1]
    gg = depth * g
    bt = lambda b: jnp.transpose(b, (3, 0, 1, 2)).reshape(cch, gg, p)
    shapes = ([jax.ShapeDtypeStruct((cch, gg, p), F32)] * 2 + [jax.ShapeDtypeStruct((SSM_CHUNK, gg, p), F32)] * 2
              + [jax.ShapeDtypeStruct((SUBLANES, gg, p), F32)] * 2)
    outs = pl.pallas_call(
        _ssm_prep_kernel, out_shape=shapes, name="ssm_prep",
    )(a_re.reshape(gg, p), a_im.reshape(gg, p), log_dt.reshape(gg, 1), bt(b_re), bt(b_im))
    per_layer = [x.reshape(x.shape[0], depth, g, p) for x in outs]
    a1, compact, coef = jax.vmap(_ssm_matrices, in_axes=(1, 1, 1, 1, 1, 1, 0, 0))(*per_layer, c_re, c_im)
    w1, w2, w3 = _ssm_expand(a1, compact, g, cch, p)
    return w1, w2, w3, coef


def _ssm_expand_kernel(a1_ref, cm_ref, w1_ref, w2_ref, w3_ref, *, groups, chans, states):
    nl = SSM_CHUNK
    lw = nl * groups * chans
    n = groups * states
    io = lambda shape, dim: lax.broadcasted_iota(jnp.int32, shape, dim)
    grp_tgc = lambda i: (i // chans) % groups
    grp_gp = lambda i: i // states
    col = io((nl * chans, lw), 1)
    e1 = jnp.where(io((nl * chans, lw), 0) == (col // (groups * chans)) * chans + col % chans, 1.0, 0.0).astype(BF16)
    e2 = jnp.where(io((states, n), 0) == io((states, n), 1) % states, 1.0, 0.0).astype(BF16)
    a1 = a1_ref[0].astype(BF16)
    m1 = grp_tgc(io((lw, lw), 0)) == grp_tgc(io((lw, lw), 1))
    w1_ref[0] = jnp.where(m1, jnp.dot(a1, e1, preferred_element_type=F32), 0.0).astype(BF16)
    m2 = grp_tgc(io((lw, n), 0)) == grp_gp(io((lw, n), 1))
    m3 = grp_gp(io((n, lw), 0)) == grp_tgc(io((n, lw), 1))
    for ri in range(2):
        x = jnp.dot(cm_ref[0, ri].astype(BF16), e2, preferred_element_type=F32)
        w2_ref[0, :, ri * n:(ri + 1) * n] = jnp.where(m2, x, 0.0).astype(BF16)
        y = lax.dot_general(e2, cm_ref[0, 2 + ri].astype(BF16), (((0,), (1,)), ((), ())),
                            preferred_element_type=F32)
        w3_ref[0, ri * n:(ri + 1) * n, :] = jnp.where(m3, y, 0.0).astype(BF16)


def _ssm_expand(a1, compact, groups, chans, states):
    depth = a1.shape[0]
    lw = SSM_CHUNK * groups * chans
    n = groups * states
    kern = functools.partial(_ssm_expand_kernel, groups=groups, chans=chans, states=states)
    return pl.pallas_call(
        kern, grid=(depth,),
        in_specs=[pl.BlockSpec((1,) + a1.shape[1:], lambda l: (l, 0, 0)),
                  pl.BlockSpec((1,) + compact.shape[1:], lambda l: (l, 0, 0, 0))],
        out_specs=[pl.BlockSpec((1, lw, lw), lambda l: (l, 0, 0)),
                   pl.BlockSpec((1, lw, 2 * n), lambda l: (l, 0, 0)),
                   pl.BlockSpec((1, 2 * n, lw), lambda l: (l, 0, 0))],
        out_shape=[jax.ShapeDtypeStruct((depth, lw, lw), BF16), jax.ShapeDtypeStruct((depth, lw, 2 * n), BF16),
                   jax.ShapeDtypeStruct((depth, 2 * n, lw), BF16)],
        compiler_params=_params("parallel"), name="ssm_expand",
    )(a1, compact)


def _ssm_matrices(bbr, bbi, p1r, p1i, pcr, pci, c_re, c_im):
    cch, g, p = bbr.shape
    w = g * cch
    n = g * p
    nl = SSM_CHUNK
    hi = lax.Precision.HIGHEST
    ar = jnp.concatenate([jnp.ones((1, g, p), F32), p1r], axis=0)
    ai = jnp.concatenate([jnp.zeros((1, g, p), F32), p1i], axis=0)
    car = c_re[None] * ar[:, :, None, :] - c_im[None] * ai[:, :, None, :]
    cai = c_re[None] * ai[:, :, None, :] + c_im[None] * ar[:, :, None, :]
    kk = (jnp.einsum('tgcp,dgp->tgcd', car[:nl], bbr, precision=hi)
          - jnp.einsum('tgcp,dgp->tgcd', cai[:nl], bbi, precision=hi))
    zk = jnp.zeros_like(kk[0])
    ksel = jnp.stack([jnp.stack([kk[t2 - t] if t2 >= t else zk for t2 in range(nl)])
                      for t in range(nl)])
    a1 = jnp.transpose(ksel, (0, 2, 4, 1, 3)).reshape(nl * w, nl * cch)
    rev_r = jnp.stack([ar[nl - 1 - t] for t in range(nl)])
    rev_i = jnp.stack([ai[nl - 1 - t] for t in range(nl)])
    abr_ = rev_r[:, None] * bbr[None] - rev_i[:, None] * bbi[None]
    abi_ = rev_r[:, None] * bbi[None] + rev_i[:, None] * bbr[None]
    rows2 = lambda x: jnp.transpose(x, (0, 2, 1, 3)).reshape(nl * w, p)
    compact = jnp.stack([rows2(abr_), rows2(abi_),
                         car[1:].reshape(nl * w, p), -cai[1:].reshape(nl * w, p)])
    pcr = pcr.reshape(SUBLANES, n)
    pci = pci.reshape(SUBLANES, n)
    row = jnp.arange(SUBLANES)[:, None]
    tabs = []
    for shift in (1, 2, 4):
        keep = row >= shift
        tabs.append(jnp.where(keep, pcr[shift - 1][None], 0.0))
        tabs.append(jnp.where(keep, pci[shift - 1][None], 0.0))
    tabs += [pcr, pci]
    return a1, compact, jnp.stack(tabs)


def _rmsnorm_rows(x, g):
    return x * lax.rsqrt(jnp.mean(x * x, axis=-1, keepdims=True) + EPS) * g


def _proj_kernel(x_ref, mod_ref, g_ref, wu_ref, wt_ref, cos_ref, sin_ref,
                 u_ref, dq_ref, dk_ref, dv_ref, mq_ref, mk_ref, mv_ref, *, n_rot, widths):
    wdq, wdk, wmq, wmk, wdv, wmv = widths
    x = x_ref[0]
    mod = mod_ref[0]
    h = _rmsnorm_rows(x, g_ref[...]) * (1.0 + mod[1:2]) + mod[0:1]
    hb = h.astype(BF16)
    u = jnp.dot(hb, wu_ref[...], preferred_element_type=F32)
    for j in range(u_ref.shape[1]):
        u_ref[0, j] = u[:, j * LANES:(j + 1) * LANES]
    pt = lax.dot_general(wt_ref[...], hb, (((1,), (1,)), ((), ())), preferred_element_type=F32)
    tm = pt.shape[1]
    rot = pt[:n_rot].reshape(n_rot // HEAD_DIM, HEAD_DIM, tm)
    cos = cos_ref[0][None]
    sin = sin_ref[0][None]
    x1 = rot[:, 0:ROPE_HALF]
    x2 = rot[:, ROPE_HALF:2 * ROPE_HALF]
    rot = jnp.concatenate([x1 * cos - x2 * sin, x2 * cos + x1 * sin, rot[:, 2 * ROPE_HALF:]], axis=1)
    rot = rot.reshape(n_rot, tm)
    qscale = HEAD_DIM ** -0.5 * LOG2_E
    o = 0
    dq = (rot[o:o + wdq] * qscale).astype(BF16); o += wdq
    dk_ref[0] = rot[o:o + wdk].T.astype(BF16); o += wdk
    mq = (rot[o:o + wmq] * qscale).astype(BF16); o += wmq
    mk_ref[0] = rot[o:o + wmk].T.astype(BF16); o += wmk
    dv = pt[o:o + wdv].astype(BF16); o += wdv
    mv = pt[o:o + wmv].astype(BF16)
    for j in range(tm // ATTN_BLOCK):
        blk = slice(j * ATTN_BLOCK, (j + 1) * ATTN_BLOCK)
        dq_ref[0, j] = dq[:, blk]
        dv_ref[0, j] = dv[:, blk]
        mq_ref[0, j] = mq[:, blk]
        mv_ref[0, j] = mv[:, blk]


def _projection(x, mod, g, wu, wt, cos_t, sin_t, widths):
    bsz, s, d = x.shape
    tm = TOKEN_TILE
    ns = s // tm
    wdq, wdk, wmq, wmk, wdv, wmv = widths
    n_rot = wdq + wdk + wmq + wmk
    ab = ATTN_BLOCK
    nab = tm // ab
    kern = functools.partial(_proj_kernel, n_rot=n_rot, widths=widths)
    tiles_t = lambda w: jax.ShapeDtypeStruct((bsz, s // ab, w, ab), BF16)
    out_shape = [
        jax.ShapeDtypeStruct((bsz, wu.shape[1] // LANES, s, LANES), F32),
        tiles_t(wdq),
        jax.ShapeDtypeStruct((bsz, s, wdk), BF16),
        tiles_t(wdv),
        tiles_t(wmq),
        jax.ShapeDtypeStruct((bsz, s, wmk), BF16),
        tiles_t(wmv),
    ]
    spec_t = lambda w: pl.BlockSpec((1, nab, w, ab), lambda b, i: (b, i, 0, 0))
    out_specs = [
        pl.BlockSpec((1, wu.shape[1] // LANES, tm, LANES), lambda b, i: (b, 0, i, 0)),
        spec_t(wdq),
        pl.BlockSpec((1, tm, wdk), lambda b, i: (b, i, 0)),
        spec_t(wdv),
        spec_t(wmq),
        pl.BlockSpec((1, tm, wmk), lambda b, i: (b, i, 0)),
        spec_t(wmv),
    ]
    in_specs = [
        pl.BlockSpec((1, tm, d), lambda b, i: (b, i, 0)),
        pl.BlockSpec((1, 6, d), lambda b, i: (b, 0, 0)),
        _const_spec((1, d)),
        _const_spec(wu.shape),
        _const_spec(wt.shape),
        pl.BlockSpec((1, ROPE_HALF, tm), lambda b, i: (b, 0, i)),
        pl.BlockSpec((1, ROPE_HALF, tm), lambda b, i: (b, 0, i)),
    ]
    return pl.pallas_call(
        kern, grid=(bsz, ns), in_specs=in_specs, out_specs=out_specs, out_shape=out_shape,
        compiler_params=_params("parallel", "parallel"), name="projection",
    )(x, mod, g, wu, wt, cos_t, sin_t)


def _gelu_tanh(y):
    return 0.5 * y * (1.0 + jnp.tanh(math.sqrt(2.0 / math.pi) * (y + 0.044715 * (y * y * y))))


def _ssm_kernel(u_ref, w1_ref, w2_ref, w3_ref, coef_ref, d_ref, gw_ref, gb_ref, ng_ref,
                y_ref, st_ref, yt_ref, carry_ref, *, n_state):
    n = n_state
    nl = SSM_CHUNK

    @pl.when(pl.program_id(1) == 0)
    def _():
        carry_ref[...] = jnp.zeros_like(carry_ref)

    ns, tm = u_ref.shape[1], u_ref.shape[2]
    w = ns * LANES
    mc = tm // nl
    uc = jnp.concatenate([u_ref[0, j, pl.ds(t, mc, stride=nl), :] for t in range(nl) for j in range(ns)],
                         axis=1).astype(BF16)
    st_ref[...] = jnp.dot(uc, w2_ref[...], preferred_element_type=F32)

    def cmul_add(xr, xi, ar, ai, sr, si):
        return xr + (ar * sr - ai * si), xi + (ar * si + ai * sr)

    first_row = lax.broadcasted_iota(jnp.int32, (SUBLANES, n), 0) == 0

    def group(r, carry):
        cr, ci = carry
        rows = pl.ds(pl.multiple_of(r * SUBLANES, SUBLANES), SUBLANES)
        xr = st_ref[rows, 0:n]
        xi = st_ref[rows, n:2 * n]
        for lvl, shift in enumerate((1, 2, 4)):
            sr = pltpu.roll(xr, shift, axis=0)
            si = pltpu.roll(xi, shift, axis=0)
            xr, xi = cmul_add(xr, xi, coef_ref[2 * lvl], coef_ref[2 * lvl + 1], sr, si)
        xr, xi = cmul_add(xr, xi, coef_ref[6], coef_ref[7], cr, ci)
        st_ref[rows, 0:n] = jnp.where(first_row, cr, pltpu.roll(xr, 1, axis=0))
        st_ref[rows, n:2 * n] = jnp.where(first_row, ci, pltpu.roll(xi, 1, axis=0))
        last = SUBLANES - 1
        return (jnp.broadcast_to(xr[last:last + 1], xr.shape),
                jnp.broadcast_to(xi[last:last + 1], xi.shape))

    cr, ci = lax.fori_loop(0, mc // SUBLANES, group, (carry_ref[0], carry_ref[1]))
    carry_ref[0] = cr
    carry_ref[1] = ci

    yc = (jnp.dot(uc, w1_ref[...], preferred_element_type=F32)
          + jnp.dot(st_ref[...].astype(BF16), w3_ref[...], preferred_element_type=F32))
    for t in range(nl):
        for j in range(ns):
            yt_ref[j, pl.ds(t, mc, stride=nl), :] = yc[:, t * w + j * LANES:t * w + (j + 1) * LANES]
    u = jnp.concatenate([u_ref[0, j] for j in range(ns)], axis=1)
    y = jnp.concatenate([yt_ref[j] for j in range(ns)], axis=1) + d_ref[...] * u
    y = _gelu_tanh(y)
    z = jnp.dot(y.astype(BF16), gw_ref[...], preferred_element_type=F32) + gb_ref[...]
    y = y * jax.nn.sigmoid(z)
    y_ref[0] = _rmsnorm_rows(y, ng_ref[...]).astype(y_ref.dtype)


def _ssm(u, w1, w2, w3, coef, d_skip, glu_w, glu_b, norm_g):
    bsz, ns, s, _ = u.shape
    w = ns * LANES
    n = coef.shape[-1]
    tm = SSM_TILE
    kern = functools.partial(_ssm_kernel, n_state=n)
    in_specs = [
        pl.BlockSpec((1, ns, tm, LANES), lambda b, i: (b, 0, i, 0)),
        _const_spec(w1.shape), _const_spec(w2.shape), _const_spec(w3.shape), _const_spec(coef.shape),
        _const_spec((1, w)), _const_spec((w, w)), _const_spec((1, w)), _const_spec((1, w)),
    ]
    return pl.pallas_call(
        kern, grid=(bsz, s // tm), in_specs=in_specs,
        out_specs=pl.BlockSpec((1, tm, w), lambda b, i: (b, i, 0)),
        out_shape=jax.ShapeDtypeStruct((bsz, s, w), BF16),
        scratch_shapes=[pltpu.VMEM((tm // SSM_CHUNK, 2 * n), F32), pltpu.VMEM((ns, tm, LANES), F32),
                        pltpu.VMEM((2, SUBLANES, n), F32)],
        compiler_params=_params("parallel", "arbitrary"), name="ssm",
    )(u, w1, w2, w3, coef, d_skip, glu_w, glu_b, norm_g)


def _row_groups(x):
    rows, lanes = x.shape
    return x.reshape(4, rows // (4 * SUBLANES), SUBLANES, lanes)


def _score_stage(k_ref, first_blk, n_gran, qz, tile_bias, s_ref, mx_ref):
    ab = ATTN_BLOCK
    for w in range(n_gran):
        kb = k_ref[0, pl.ds(pl.multiple_of((first_blk + w) * ab, ab), ab), :]
        s = jnp.dot(kb, qz, preferred_element_type=F32)
        tb = tile_bias(w)
        if tb is not None:
            s = s + tb
        s_ref[w * ab:(w + 1) * ab, :] = s
        mx_ref[w] = jnp.max(jnp.max(_row_groups(s), axis=1), axis=0)


def _softmax_stage(s_ref, mx_ref, vt_ref, first_blk, n_gran, v_rows, row_bias, m, l, acc_ref):
    ab = ATTN_BLOCK
    rbs = [row_bias(w) for w in range(n_gran)]
    raw = mx_ref[0]
    seen = raw if rbs[0] is None else raw + rbs[0]
    for w in range(1, n_gran):
        raw = jnp.maximum(raw, mx_ref[w])
        seen = jnp.maximum(seen, mx_ref[w] if rbs[w] is None else mx_ref[w] + rbs[w])
    m_new = jnp.maximum(m, jnp.max(seen, axis=0, keepdims=True))
    shift = jnp.where(m_new > 0.5 * NEG, m_new, jnp.max(raw, axis=0, keepdims=True))
    alpha = jnp.exp2(m - m_new)
    psum = None
    pv = None
    for w in range(n_gran):
        p = jnp.exp2(s_ref[w * ab:(w + 1) * ab, :] - (shift if rbs[w] is None else shift - rbs[w]))
        part = jnp.sum(jnp.sum(_row_groups(p), axis=1), axis=0)
        psum = part if psum is None else psum + part
        d = jnp.dot(vt_ref[0, first_blk + w, v_rows, :], p.astype(BF16), preferred_element_type=F32)
        pv = d if pv is None else pv + d
    acc_ref[...] = alpha * acc_ref[...] + pv
    return m_new, alpha * l + jnp.sum(psum, axis=0, keepdims=True)


def _flash_tile(g, k_ref, vt_ref, chains, full_bias, diag_bias, qz_ref, s_ref, mx_ref, acc_ref):
    nc = len(chains)
    t = s_ref.shape[-1]
    acc_ref[...] = jnp.zeros(acc_ref.shape, F32)
    neg = jnp.full((1, t), NEG, F32)
    zero = jnp.zeros((1, t), F32)

    def score(c, first, n_gran, bias):
        _score_stage(k_ref, first, n_gran, qz_ref[c], bias[0], s_ref.at[c], mx_ref.at[c])

    def softmax(c, first, n_gran, bias, m, l):
        return _softmax_stage(s_ref.at[c], mx_ref.at[c], vt_ref, first, n_gran, chains[c][1], bias[1],
                              m, l, acc_ref.at[c])

    score(0, 0, ATTN_CHUNK, full_bias(0, 0))

    def step(i, carry):
        first = i * ATTN_CHUNK
        out = []
        for c in range(nc):
            if c + 1 < nc:
                score(c + 1, first, ATTN_CHUNK, full_bias(c + 1, first))
            else:
                score(0, first + ATTN_CHUNK, ATTN_CHUNK, full_bias(0, first + ATTN_CHUNK))
            out += softmax(c, first, ATTN_CHUNK, full_bias(c, first), carry[2 * c], carry[2 * c + 1])
        return tuple(out)

    carry = lax.fori_loop(0, g, step, (neg, zero) * nc)
    first = g * ATTN_CHUNK
    out = []
    score(0, first, chains[0][0] + 1, diag_bias(0))
    for c in range(nc):
        if c + 1 < nc:
            score(c + 1, first, chains[c + 1][0] + 1, diag_bias(c + 1))
        out.append(softmax(c, first, chains[c][0] + 1, diag_bias(c), carry[2 * c], carry[2 * c + 1]))
    return out


def _flash_scratch(nc, dv):
    ab = ATTN_BLOCK
    return [pltpu.VMEM((nc, 2 * HEAD_DIM, ab), BF16),
            pltpu.VMEM((nc, ATTN_CHUNK * ab, ab), F32),
            pltpu.VMEM((nc, ATTN_CHUNK, SUBLANES, ab), F32),
            pltpu.VMEM((nc, dv, ab), F32),
            pltpu.VMEM((ab, ab), F32)]


def _causal_bias(t):
    key = lax.broadcasted_iota(jnp.int32, (t, t), 0)
    qry = lax.broadcasted_iota(jnp.int32, (t, t), 1)
    return jnp.where(key <= qry, 0.0, NEG)


def _diff_kernel(lam_ref, g_ref, qt_ref, k_ref, vt_ref, o_ref, qz_ref, s_ref, mx_ref, acc_ref, tri_ref,
                 *, lam_init):
    g = pl.program_id(2)
    ab = ATTN_BLOCK
    chains = [(qb, slice(None)) for qb in range(ATTN_CHUNK) for _ in range(2)]
    tri_ref[...] = _causal_bias(ab)
    row = lax.broadcasted_iota(jnp.int32, (2 * HEAD_DIM, ab), 0)
    for qb in range(ATTN_CHUNK):
        qt = qt_ref[0, qb].astype(F32)
        qz_ref[2 * qb] = jnp.where(row < HEAD_DIM, qt, 0.0).astype(BF16)
        qz_ref[2 * qb + 1] = jnp.where(row >= HEAD_DIM, qt, 0.0).astype(BF16)

    none = lambda w: None

    def diag_bias(c):
        qb = chains[c][0]
        return (lambda w: tri_ref[...] if w == qb else None), none

    ml = _flash_tile(g, k_ref, vt_ref, chains, lambda c, first: (none, none), diag_bias,
                     qz_ref, s_ref, mx_ref, acc_ref)

    lp = lam_ref[...]
    lam = (jnp.exp(jnp.sum(lp[0:1] * lp[1:2], axis=1, keepdims=True))
           - jnp.exp(jnp.sum(lp[2:3] * lp[3:4], axis=1, keepdims=True)) + lam_init)
    for qb in range(ATTN_CHUNK):
        o = acc_ref[2 * qb] / ml[2 * qb][1] - lam * (acc_ref[2 * qb + 1] / ml[2 * qb + 1][1])
        o = o * lax.rsqrt(jnp.mean(o * o, axis=0, keepdims=True) + EPS) * g_ref[...]
        o_ref[0, qb * ab:(qb + 1) * ab, :] = (o * (1.0 - lam_init)).T.astype(o_ref.dtype)


def _diff_attention(qt, k, vt, lam_params, subln_g, lam_init):
    bsz, nb, wq, ab = qt.shape
    s = k.shape[1]
    dv = 2 * HEAD_DIM
    nh = wq // dv
    tq = ATTN_CHUNK * ab
    kern = functools.partial(_diff_kernel, lam_init=lam_init)
    in_specs = [
        _const_spec(lam_params.shape),
        _const_spec((dv, 1)),
        pl.BlockSpec((1, ATTN_CHUNK, dv, ab), lambda b, h, i: (b, i, h, 0)),
        pl.BlockSpec((1, s, dv), lambda b, h, i: (b, 0, h)),
        pl.BlockSpec((1, nb, dv, ab), lambda b, h, i: (b, 0, h, 0)),
    ]
    return pl.pallas_call(
        kern, grid=(bsz, nh, nb // ATTN_CHUNK), in_specs=in_specs,
        out_specs=pl.BlockSpec((1, tq, dv), lambda b, h, i: (b, i, h)),
        out_shape=jax.ShapeDtypeStruct((bsz, s, wq), BF16),
        scratch_shapes=_flash_scratch(2 * ATTN_CHUNK, dv),
        compiler_params=_params("parallel", "parallel", "arbitrary"), name="diff_attention",
    )(lam_params, subln_g.reshape(dv, 1), qt, k, vt)


def _moba_kernel(g_ref, qt_ref, k_ref, vt_ref, o_ref, kmean_ref, bias_ref, qz_ref, s_ref, mx_ref, acc_ref,
                 tri_ref, *, n_blocks):
    g = pl.program_id(2)
    ab = ATTN_BLOCK
    tq = ATTN_CHUNK * ab
    pair = 2 * HEAD_DIM

    @pl.when(g == 0)
    def _():
        def mean_block(j, c):
            kb = k_ref[0, pl.ds(pl.multiple_of(j * ab, ab), ab), :].astype(F32)
            kmean_ref[pl.ds(j, 1), :] = jnp.mean(kb, axis=0, keepdims=True)
            return c
        lax.fori_loop(0, n_blocks, mean_block, 0)

    tri_ref[...] = _causal_bias(ab)
    chains = [(qb, slice(hh * HEAD_DIM, (hh + 1) * HEAD_DIM)) for qb in range(ATTN_CHUNK) for hh in range(2)]
    qt = jnp.concatenate([qt_ref[0, qb] for qb in range(ATTN_CHUNK)], axis=1).astype(F32)
    row = lax.broadcasted_iota(jnp.int32, (pair, tq), 0)
    lane = lax.broadcasted_iota(jnp.int32, (n_blocks, pair), 1)
    bid = lax.broadcasted_iota(jnp.int32, (n_blocks, tq), 0)
    qblk = g * ATTN_CHUNK + lax.broadcasted_iota(jnp.int32, (n_blocks, tq), 1) // ab
    past = bid < qblk
    bid = bid.astype(F32)
    kmean = kmean_ref[...]
    for hh in range(2):
        qf = jnp.where((row >= hh * HEAD_DIM) & (row < (hh + 1) * HEAD_DIM), qt, 0.0)
        for qb in range(ATTN_CHUNK):
            qz_ref[2 * qb + hh] = qf[:, qb * ab:(qb + 1) * ab].astype(BF16)
        km = jnp.where((lane >= hh * HEAD_DIM) & (lane < (hh + 1) * HEAD_DIM), kmean, 0.0)
        gate = jnp.dot(km, qf, preferred_element_type=F32, precision=lax.Precision.HIGHEST)
        gate = jnp.where(past, gate, NEG)
        sel = jnp.zeros_like(gate)
        for _ in range(MOBA_TOPK):
            top = jnp.max(gate, axis=0, keepdims=True)
            first = jnp.min(jnp.where(gate == top, bid, float(n_blocks)), axis=0, keepdims=True)
            hit = bid == first
            sel = jnp.where(hit, 1.0, sel)
            gate = jnp.where(hit, BELOW_NEG, gate)
        bias_ref[hh] = jnp.where(past, jnp.where(sel > 0.0, 0.0, NEG), NEG)

    none = lambda w: None

    def picked(c, blk):
        qb, hh = c // 2, c % 2
        return bias_ref[hh, pl.ds(blk, 1), qb * ab:(qb + 1) * ab]

    def full_bias(c, first):
        return none, lambda w: picked(c, first + w)

    def diag_bias(c):
        qb = chains[c][0]
        return ((lambda w: tri_ref[...] if w == qb else None),
                (lambda w: None if w == qb else picked(c, g * ATTN_CHUNK + w)))

    ml = _flash_tile(g, k_ref, vt_ref, chains, full_bias, diag_bias, qz_ref, s_ref, mx_ref, acc_ref)

    for qb in range(ATTN_CHUNK):
        outs = []
        for hh in range(2):
            o = acc_ref[2 * qb + hh] / ml[2 * qb + hh][1]
            outs.append(o * lax.rsqrt(jnp.mean(o * o, axis=0, keepdims=True) + EPS) * g_ref[...])
        o_ref[0, qb * ab:(qb + 1) * ab, :] = jnp.concatenate(outs, axis=0).T.astype(o_ref.dtype)


def _moba_attention(qt, k, vt, norm_g):
    bsz, nb, wq, ab = qt.shape
    s = k.shape[1]
    pair = 2 * HEAD_DIM
    tq = ATTN_CHUNK * ab
    kern = functools.partial(_moba_kernel, n_blocks=nb)
    in_specs = [
        _const_spec((HEAD_DIM, 1)),
        pl.BlockSpec((1, ATTN_CHUNK, pair, ab), lambda b, h, i: (b, i, h, 0)),
        pl.BlockSpec((1, s, pair), lambda b, h, i: (b, 0, h)),
        pl.BlockSpec((1, nb, pair, ab), lambda b, h, i: (b, 0, h, 0)),
    ]
    return pl.pallas_call(
        kern, grid=(bsz, wq // pair, nb // ATTN_CHUNK), in_specs=in_specs,
        out_specs=pl.BlockSpec((1, tq, pair), lambda b, h, i: (b, i, h)),
        out_shape=jax.ShapeDtypeStruct((bsz, s, wq), BF16),
        scratch_shapes=[pltpu.VMEM((nb, pair), F32), pltpu.VMEM((2, nb, tq), F32)]
        + _flash_scratch(2 * ATTN_CHUNK, HEAD_DIM),
        compiler_params=_params("parallel", "parallel", "arbitrary"), name="moba_attention",
    )(norm_g.reshape(HEAD_DIM, 1), qt, k, vt)


def _out_mlp_kernel(x_ref, ys_ref, yd_ref, ym_ref, mod_ref, g2_ref, wo_ref, w1_ref, w2_ref, fg_ref,
                    o_ref, *, final_norm, ff_chunk):
    x = x_ref[0]
    mod = mod_ref[0]
    ws = ys_ref.shape[2]
    wd = yd_ref.shape[2]
    mix = (jnp.dot(ys_ref[0], wo_ref[0:ws], preferred_element_type=F32)
           + jnp.dot(yd_ref[0], wo_ref[ws:ws + wd], preferred_element_type=F32)
           + jnp.dot(ym_ref[0], wo_ref[ws + wd:], preferred_element_type=F32))
    x = x + mod[2:3] * mix
    h = (_rmsnorm_rows(x, g2_ref[...]) * (1.0 + mod[4:5]) + mod[3:4]).astype(BF16)
    acc = jnp.zeros_like(x)
    for c in range(w1_ref.shape[1] // ff_chunk):
        a = jnp.dot(h, w1_ref[:, c * ff_chunk:(c + 1) * ff_chunk], preferred_element_type=F32)
        a = jnp.square(jnp.maximum(a, 0.0)).astype(BF16)
        acc = acc + jnp.dot(a, w2_ref[c * ff_chunk:(c + 1) * ff_chunk, :], preferred_element_type=F32)
    x = x + mod[5:6] * acc
    if final_norm:
        x = _rmsnorm_rows(x, fg_ref[...])
    o_ref[0] = x


def _out_mlp(x, ys, yd, ym, mod, g2, wo, w1, w2, final_g, final_norm):
    bsz, s, d = x.shape
    tm = TOKEN_TILE
    kern = functools.partial(_out_mlp_kernel, final_norm=final_norm, ff_chunk=d)
    tok = lambda w: pl.BlockSpec((1, tm, w), lambda b, i: (b, i, 0))
    in_specs = [
        tok(d), tok(ys.shape[2]), tok(yd.shape[2]), tok(ym.shape[2]),
        pl.BlockSpec((1, 6, d), lambda b, i: (b, 0, 0)),
        _const_spec((1, d)), _const_spec(wo.shape), _const_spec(w1.shape), _const_spec(w2.shape),
        _const_spec((1, d)),
    ]
    return pl.pallas_call(
        kern, grid=(bsz, s // tm), in_specs=in_specs, out_specs=tok(d),
        out_shape=jax.ShapeDtypeStruct((bsz, s, d), F32),
        compiler_params=_params("parallel", "parallel"), name="out_mlp",
    )(x, ys, yd, ym, mod, g2, wo, w1, w2, final_g)


def kernel(x, c, positions, norm1_g, norm2_g, w_ada, b_ada, w_in, w_out, ssm_a_re, ssm_a_im, ssm_log_dt, ssm_b_re, ssm_b_im, ssm_c_re, ssm_c_im, ssm_d, ssm_glu_w, ssm_glu_b, ssm_norm_g, diff_lq1, diff_lk1, diff_lq2, diff_lk2, diff_subln_g, moba_norm_g, mlp_w1, mlp_w2, final_g):
    bsz, s, d = x.shape
    depth = w_in.shape[0]
    w_ssm = ssm_d.shape[1]
    w_diff = DIFF_HEADS * 2 * HEAD_DIM
    w_moba = MOBA_HEADS * HEAD_DIM
    assert s % TOKEN_TILE == 0 and TOKEN_TILE % ATTN_BLOCK == 0 and s % (ATTN_CHUNK * ATTN_BLOCK) == 0
    assert s % SSM_TILE == 0 and SSM_TILE % (SSM_CHUNK * SUBLANES) == 0
    assert w_in.shape[2] == w_ssm + 3 * w_diff + 3 * w_moba

    cos_t, sin_t = _rope_tables(positions)
    mod_all = _ada_mod(c, w_ada, b_ada).reshape(depth, bsz, 6, d)
    sw1, sw2, sw3, scoef = _ssm_prep(ssm_a_re, ssm_a_im, ssm_log_dt, ssm_b_re, ssm_b_im, ssm_c_re, ssm_c_im)

    o_dq = w_ssm
    o_dk = o_dq + w_diff
    o_dv = o_dk + w_diff
    o_mq = o_dv + w_diff
    o_mk = o_mq + w_moba
    o_mv = o_mk + w_moba
    widths = (w_diff, w_diff, w_moba, w_moba, w_diff, w_moba)

    for l in range(depth):
        wl = w_in[l]
        wu = wl[:, :o_dq].astype(BF16)
        wf = wl.T.astype(BF16)
        wt = jnp.concatenate([wf[o_dq:o_dk], wf[o_dk:o_dv], wf[o_mq:o_mk], wf[o_mk:o_mv],
                              wf[o_dv:o_mq], wf[o_mv:]], axis=0)
        mod = mod_all[l]
        u, dq, dk, dv, mq, mk, mv = _projection(x, mod, norm1_g[l][None], wu, wt, cos_t, sin_t, widths)

        y_ssm = _ssm(u, sw1[l], sw2[l], sw3[l], scoef[l], ssm_d[l][None], ssm_glu_w[l].astype(BF16),
                     ssm_glu_b[l][None], ssm_norm_g[l][None])

        lam_init = 0.8 - 0.6 * math.exp(-0.3 * l)
        lam_params = jnp.stack([diff_lq1[l], diff_lk1[l], diff_lq2[l], diff_lk2[l]]).astype(F32)
        y_diff = _diff_attention(dq, dk, dv, lam_params, diff_subln_g[l], lam_init)
        y_moba = _moba_attention(mq, mk, mv, moba_norm_g[l])

        x = _out_mlp(x, y_ssm, y_diff, y_moba, mod, norm2_g[l][None], w_out[l].astype(BF16),
                     mlp_w1[l].astype(BF16), mlp_w2[l].astype(BF16), final_g[None],
                     final_norm=(l == depth - 1))
    return x
```
